```python
import math
import jax, jax.numpy as jnp
from jax import lax
import numpy as np

D_MODEL = 1024
BATCH = 8
SEQ = 2048
DEPTH = 4
DEC_BATCH = 128
DEC_SEQ = 8
PAST_LEN = 16384
PAGE_SIZE = 128

N_MIXERS = 2
N_S5_LAYERS = (DEPTH + 1) // 2
N_RWKV_LAYERS = DEPTH // 2
S5_GROUP = 16
S5_GROUPS = D_MODEL // S5_GROUP
S5_STATE = 64
RWKV_HEAD = 64
RWKV_HEADS = D_MODEL // RWKV_HEAD
DECAY_LORA = 64
AAA_LORA = 64
MV_LORA = 32
GATE_LORA = 128
RWKV_GN_EPS = 64e-5
MEM_TOKENS = 256
XA_HEADS = 4
XA_HEAD_DIM = D_MODEL // XA_HEADS
D_FF = 4 * D_MODEL
LN_EPS = 1e-5
DEEPNORM_ALPHA = (2.0 * DEPTH) ** 0.25
DEEPNORM_BETA = (8.0 * DEPTH) ** -0.25

kernel_name = "s5_rwkv7_memxattn_deepnorm_step"


def layer_norm(x, g, b):
    xf = x.astype(jnp.float32)
    mu = jnp.mean(xf, axis=-1, keepdims=True)
    var = jnp.mean(jnp.square(xf - mu), axis=-1, keepdims=True)
    return ((xf - mu) * lax.rsqrt(var + LN_EPS) * g + b).astype(x.dtype)


def _complex_affine_combine(earlier, later):
    a1r, a1i, b1r, b1i = earlier
    a2r, a2i, b2r, b2i = later
    ar = a2r * a1r - a2i * a1i
    ai = a2r * a1i + a2i * a1r
    br = a2r * b1r - a2i * b1i + b2r
    bi = a2r * b1i + a2i * b1r + b2i
    return (ar, ai, br, bi)


def s5_mixer(u, h0_re, h0_im, p, j):
    f32 = jnp.float32
    bsz, t_len, _ = u.shape
    uf = u.astype(f32)
    ug = uf.reshape(bsz, t_len, S5_GROUPS, S5_GROUP)
    lam_re = p["s5_a_re"][j].astype(f32)
    lam_im = p["s5_a_im"][j].astype(f32)
    dt = jnp.exp(p["s5_log_dt"][j].astype(f32))[:, None]
    mag = jnp.exp(lam_re * dt)
    ang = lam_im * dt
    ab_re = mag * jnp.cos(ang)
    ab_im = mag * jnp.sin(ang)
    den = lam_re * lam_re + lam_im * lam_im
    num_re = ab_re - 1.0
    f_re = (num_re * lam_re + ab_im * lam_im) / den
    f_im = (ab_im * lam_re - num_re * lam_im) / den
    br = p["s5_b_re"][j].astype(f32)
    bi = p["s5_b_im"][j].astype(f32)
    bb_re = f_re[..., None] * br - f_im[..., None] * bi
    bb_im = f_re[..., None] * bi + f_im[..., None] * br
    bu_re = jnp.einsum('btgc,gnc->tbgn', ug, bb_re)
    bu_im = jnp.einsum('btgc,gnc->tbgn', ug, bb_im)
    a_t_re = jnp.broadcast_to(ab_re, (t_len, 1, S5_GROUPS, S5_STATE))
    a_t_im = jnp.broadcast_to(ab_im, (t_len, 1, S5_GROUPS, S5_STATE))
    p_re, p_im, hs_re, hs_im = lax.associative_scan(
        _complex_affine_combine, (a_t_re, a_t_im, bu_re, bu_im), axis=0)
    h0r = h0_re.astype(f32)[None]
    h0i = h0_im.astype(f32)[None]
    h_re = hs_re + p_re * h0r - p_im * h0i
    h_im = hs_im + p_re * h0i + p_im * h0r
    cr = p["s5_c_re"][j].astype(f32)
    ci = p["s5_c_im"][j].astype(f32)
    y = jnp.einsum('tbgn,gcn->btgc', h_re, cr) - jnp.einsum('tbgn,gcn->btgc', h_im, ci)
    y = y.reshape(bsz, t_len, D_MODEL) + p["s5_d"][j].astype(f32) * uf
    z = jax.nn.gelu(y).astype(u.dtype)
    out = (z @ p["s5_w_glu_v"][j]) * jax.nn.sigmoid(z @ p["s5_w_glu_g"][j])
    return out.astype(u.dtype), h_re[-1], h_im[-1]


def _rwkv7_step(S, inp):
    r_t, w_t, k_t, v_t, kk_t, a_t = inp
    sa = jnp.einsum('bhij,bhj->bhi', S, -kk_t)
    S = (S * w_t[:, :, None, :]
         + sa[..., None] * (kk_t * a_t)[:, :, None, :]
         + v_t[..., None] * k_t[:, :, None, :])
    y = jnp.einsum('bhij,bhj->bhi', S, r_t)
    return S, y


def rwkv7_mixer(x, s0, x_prev, v_first, p, j):
    f32 = jnp.float32
    bsz, t_len, _ = x.shape
    xx = jnp.concatenate([x_prev[:, None].astype(x.dtype), x[:, :-1]], axis=1) - x
    mu = p["rwkv_mu"][j]
    xr = x + xx * mu[0]
    xw = x + xx * mu[1]
    xk = x + xx * mu[2]
    xv = x + xx * mu[3]
    xa = x + xx * mu[4]
    xg = x + xx * mu[5]
    r = xr @ p["rwkv_w_r"][j]
    k = xk @ p["rwkv_w_k"][j]
    v = xv @ p["rwkv_w_v"][j]
    w_log = -jax.nn.softplus(-(p["rwkv_w0"][j] + jnp.tanh(xw @ p["rwkv_w1"][j]) @ p["rwkv_w2"][j])) - 0.5
    if j == 0:
        v_first = v
    else:
        gate_v = jax.nn.sigmoid(p["rwkv_v0"][j - 1] + (xv @ p["rwkv_v1"][j - 1]) @ p["rwkv_v2"][j - 1])
        v = v + (v_first - v) * gate_v
    a = jax.nn.sigmoid(p["rwkv_a0"][j] + (xa @ p["rwkv_a1"][j]) @ p["rwkv_a2"][j])
    g = jax.nn.sigmoid(xg @ p["rwkv_g1"][j]) @ p["rwkv_g2"][j]
    hshape = (bsz, t_len, RWKV_HEADS, RWKV_HEAD)
    kk = (k * p["rwkv_k_k"][j]).astype(f32).reshape(hshape)
    kk = kk / jnp.maximum(jnp.sqrt(jnp.sum(kk * kk, axis=-1, keepdims=True)), 1e-12)
    k = k * (1.0 + (a - 1.0) * p["rwkv_k_a"][j])
    decay = jnp.exp(-jnp.exp(w_log.astype(f32)))
    r_h = r.astype(f32).reshape(hshape)
    k_h = k.astype(f32).reshape(hshape)
    v_h = v.astype(f32).reshape(hshape)
    a_h = a.astype(f32).reshape(hshape)
    w_h = decay.reshape(hshape)
    tm = lambda t: jnp.moveaxis(t, 1, 0)
    S_last, ys = lax.scan(_rwkv7_step, s0.astype(f32),
                          (tm(r_h), tm(w_h), tm(k_h), tm(v_h), tm(kk), tm(a_h)))
    y = jnp.moveaxis(ys, 0, 1)
    mu_y = jnp.mean(y, axis=-1, keepdims=True)
    var_y = jnp.mean(jnp.square(y - mu_y), axis=-1, keepdims=True)
    y = ((y - mu_y) * lax.rsqrt(var_y + RWKV_GN_EPS)).reshape(bsz, t_len, D_MODEL)
    y = y * p["rwkv_lnx_g"][j] + p["rwkv_lnx_b"][j]
    bonus = jnp.sum(r_h * k_h * p["rwkv_r_k"][j].astype(f32), axis=-1, keepdims=True) * v_h
    o = ((y + bonus.reshape(bsz, t_len, D_MODEL)) * g).astype(x.dtype)
    out = o @ p["rwkv_w_o"][j]
    return out.astype(x.dtype), S_last, x[:, -1], v_first


def memory_cross_attention(x, mk, mv, w_q, w_o):
    bsz, t_len, _ = x.shape
    q = (x @ w_q).reshape(bsz, t_len, XA_HEADS, XA_HEAD_DIM)
    s = jnp.einsum('bthd,bmhd->bhtm', q, mk).astype(jnp.float32) * (XA_HEAD_DIM ** -0.5)
    pr = jax.nn.softmax(s, axis=-1).astype(x.dtype)
    o = jnp.einsum('bhtm,bmhd->bthd', pr, mv).reshape(bsz, t_len, D_MODEL).astype(x.dtype)
    return (o @ w_o).astype(x.dtype)


def sq_relu_mlp(x, w1, w2):
    h = jax.nn.relu(x @ w1)
    return ((h * h) @ w2).astype(x.dtype)


def trunk(x, mem_k, mem_v, s5_re, s5_im, rwkv_s, shift, p):
    new_re, new_im, new_s, new_shift = [], [], [], []
    v_first = None
    for i in range(DEPTH):
        j = i // N_MIXERS
        if i % N_MIXERS == 0:
            h, hr, hi = s5_mixer(x, s5_re[j], s5_im[j], p, j)
            new_re.append(hr)
            new_im.append(hi)
        else:
            h, S_last, x_last, v_first = rwkv7_mixer(x, rwkv_s[j], shift[j], v_first, p, j)
            new_s.append(S_last)
            new_shift.append(x_last)
        x = layer_norm(DEEPNORM_ALPHA * x + h, p["ln_g"][i, 0], p["ln_b"][i, 0])
        h = memory_cross_attention(x, mem_k[i], mem_v[i], p["xa_w_q"][i], p["xa_w_o"][i])
        x = layer_norm(DEEPNORM_ALPHA * x + h, p["ln_g"][i, 1], p["ln_b"][i, 1])
        h = sq_relu_mlp(x, p["mlp_w1"][i], p["mlp_w2"][i])
        x = layer_norm(DEEPNORM_ALPHA * x + h, p["ln_g"][i, 2], p["ln_b"][i, 2])
    return x, jnp.stack(new_re), jnp.stack(new_im), jnp.stack(new_s), jnp.stack(new_shift)


def setup_inputs(seed: int = 0) -> dict:
    key = jax.random.key(seed)
    ks = iter(jax.random.split(key, 64))
    f32 = jnp.float32

    def nrm(shape, scale):
        return scale * jax.random.normal(next(ks), shape, f32)

    NR, NS, NV = N_RWKV_LAYERS, N_S5_LAYERS, N_RWKV_LAYERS - 1
    D = D_MODEL
    inp = {}
    inp["x_prompt"] = nrm((BATCH, SEQ, D), 1.0)
    inp["x_sample"] = nrm((DEC_BATCH, DEC_SEQ, D), 1.0)
    inp["mem_prompt"] = nrm((BATCH, MEM_TOKENS, D), 1.0)
    inp["cache_mem_k"] = nrm((DEPTH, DEC_BATCH, MEM_TOKENS, XA_HEADS, XA_HEAD_DIM), 1.0)
    inp["cache_mem_v"] = nrm((DEPTH, DEC_BATCH, MEM_TOKENS, XA_HEADS, XA_HEAD_DIM), 1.0)
    inp["state_s5_re"] = nrm((NS, DEC_BATCH, S5_GROUPS, S5_STATE), 0.5)
    inp["state_s5_im"] = nrm((NS, DEC_BATCH, S5_GROUPS, S5_STATE), 0.5)
    inp["state_rwkv"] = nrm((NR, DEC_BATCH, RWKV_HEADS, RWKV_HEAD, RWKV_HEAD), 0.2)
    inp["state_shift"] = nrm((NR, DEC_BATCH, D), 1.0)
    inp["ln_g"] = 1.0 + nrm((DEPTH, 3, D), 0.01)
    inp["ln_b"] = nrm((DEPTH, 3, D), 0.01)
    inp["s5_a_re"] = -0.5 + nrm((NS, S5_GROUPS, S5_STATE), 0.01)
    inp["s5_a_im"] = math.pi * jnp.arange(S5_STATE, dtype=f32) + nrm((NS, S5_GROUPS, S5_STATE), 0.01)
    inp["s5_log_dt"] = jax.random.uniform(next(ks), (NS, S5_GROUPS), f32, math.log(1e-3), math.log(1e-1))
    inp["s5_b_re"] = nrm((NS, S5_GROUPS, S5_STATE, S5_GROUP), (2.0 * S5_GROUP) ** -0.5)
    inp["s5_b_im"] = nrm((NS, S5_GROUPS, S5_STATE, S5_GROUP), (2.0 * S5_GROUP) ** -0.5)
    inp["s5_c_re"] = nrm((NS, S5_GROUPS, S5_GROUP, S5_STATE), (2.0 * S5_STATE) ** -0.5)
    inp["s5_c_im"] = nrm((NS, S5_GROUPS, S5_GROUP, S5_STATE), (2.0 * S5_STATE) ** -0.5)
    inp["s5_d"] = nrm((NS, D), 1.0)
    inp["s5_w_glu_v"] = nrm((NS, D, D), DEEPNORM_BETA * D ** -0.5)
    inp["s5_w_glu_g"] = nrm((NS, D, D), D ** -0.5)
    inp["rwkv_mu"] = jax.random.uniform(next(ks), (NR, 6, D), f32)
    inp["rwkv_w_r"] = nrm((NR, D, D), D ** -0.5)
    inp["rwkv_w_k"] = nrm((NR, D, D), D ** -0.5)
    inp["rwkv_w_v"] = nrm((NR, D, D), D ** -0.5)
    inp["rwkv_w_o"] = nrm((NR, D, D), DEEPNORM_BETA * D ** -0.5)
    inp["rwkv_w0"] = jnp.linspace(-6.0, -1.0, D, dtype=f32)[None] + nrm((NR, D), 0.1)
    inp["rwkv_w1"] = nrm((NR, D, DECAY_LORA), D ** -0.5)
    inp["rwkv_w2"] = nrm((NR, DECAY_LORA, D), 0.1 * DECAY_LORA ** -0.5)
    inp["rwkv_a0"] = nrm((NR, D), 0.1)
    inp["rwkv_a1"] = nrm((NR, D, AAA_LORA), D ** -0.5)
    inp["rwkv_a2"] = nrm((NR, AAA_LORA, D), 0.1 * AAA_LORA ** -0.5)
    inp["rwkv_v0"] = 1.0 + nrm((NV, D), 0.1)
    inp["rwkv_v1"] = nrm((NV, D, MV_LORA), D ** -0.5)
    inp["rwkv_v2"] = nrm((NV, MV_LORA, D), 0.1 * MV_LORA ** -0.5)
    inp["rwkv_g1"] = nrm((NR, D, GATE_LORA), D ** -0.5)
    inp["rwkv_g2"] = nrm((NR, GATE_LORA, D), GATE_LORA ** -0.5)
    inp["rwkv_k_k"] = 0.85 + nrm((NR, D), 0.05)
    inp["rwkv_k_a"] = 1.0 + nrm((NR, D), 0.05)
    inp["rwkv_r_k"] = nrm((NR, RWKV_HEADS, RWKV_HEAD), 0.1)
    inp["rwkv_lnx_g"] = 1.0 + nrm((NR, D), 0.01)
    inp["rwkv_lnx_b"] = nrm((NR, D), 0.01)
    inp["xa_w_q"] = nrm((DEPTH, D, D), D ** -0.5)
    inp["xa_w_k"] = nrm((DEPTH, D, D), D ** -0.5)
    inp["xa_w_v"] = nrm((DEPTH, D, D), D ** -0.5)
    inp["xa_w_o"] = nrm((DEPTH, D, D), DEEPNORM_BETA * D ** -0.5)
    inp["mlp_w1"] = nrm((DEPTH, D, D_FF), D ** -0.5)
    inp["mlp_w2"] = nrm((DEPTH, D_FF, D), DEEPNORM_BETA * D_FF ** -0.5)
    return inp


def reference(x_prompt, x_sample, mem_prompt, cache_mem_k, cache_mem_v, state_s5_re, state_s5_im,
              state_rwkv, state_shift, ln_g, ln_b, s5_a_re, s5_a_im, s5_log_dt, s5_b_re, s5_b_im,
              s5_c_re, s5_c_im, s5_d, s5_w_glu_v, s5_w_glu_g, rwkv_mu, rwkv_w_r, rwkv_w_k, rwkv_w_v,
              rwkv_w_o, rwkv_w0, rwkv_w1, rwkv_w2, rwkv_a0, rwkv_a1, rwkv_a2, rwkv_v0, rwkv_v1, rwkv_v2,
              rwkv_g1, rwkv_g2, rwkv_k_k, rwkv_k_a, rwkv_r_k, rwkv_lnx_g, rwkv_lnx_b,
              xa_w_q, xa_w_k, xa_w_v, xa_w_o, mlp_w1, mlp_w2):
    p = {
        "ln_g": ln_g, "ln_b": ln_b,
        "s5_a_re": s5_a_re, "s5_a_im": s5_a_im, "s5_log_dt": s5_log_dt,
        "s5_b_re": s5_b_re, "s5_b_im": s5_b_im, "s5_c_re": s5_c_re, "s5_c_im": s5_c_im,
        "s5_d": s5_d, "s5_w_glu_v": s5_w_glu_v, "s5_w_glu_g": s5_w_glu_g,
        "rwkv_mu": rwkv_mu, "rwkv_w_r": rwkv_w_r, "rwkv_w_k": rwkv_w_k, "rwkv_w_v": rwkv_w_v,
        "rwkv_w_o": rwkv_w_o, "rwkv_w0": rwkv_w0, "rwkv_w1": rwkv_w1, "rwkv_w2": rwkv_w2,
        "rwkv_a0": rwkv_a0, "rwkv_a1": rwkv_a1, "rwkv_a2": rwkv_a2,
        "rwkv_v0": rwkv_v0, "rwkv_v1": rwkv_v1, "rwkv_v2": rwkv_v2,
        "rwkv_g1": rwkv_g1, "rwkv_g2": rwkv_g2, "rwkv_k_k": rwkv_k_k, "rwkv_k_a": rwkv_k_a,
        "rwkv_r_k": rwkv_r_k, "rwkv_lnx_g": rwkv_lnx_g, "rwkv_lnx_b": rwkv_lnx_b,
        "xa_w_q": xa_w_q, "xa_w_o": xa_w_o, "mlp_w1": mlp_w1, "mlp_w2": mlp_w2,
    }
    f32 = jnp.float32
    b_p = x_prompt.shape[0]
    mk_shape = (DEPTH, b_p, mem_prompt.shape[1], XA_HEADS, XA_HEAD_DIM)
    mem_k_prompt = jnp.einsum('bmd,ldc->lbmc', mem_prompt, xa_w_k).reshape(mk_shape)
    mem_v_prompt = jnp.einsum('bmd,ldc->lbmc', mem_prompt, xa_w_v).reshape(mk_shape)
    z_s5 = jnp.zeros((N_S5_LAYERS, b_p, S5_GROUPS, S5_STATE), f32)
    z_rwkv = jnp.zeros((N_RWKV_LAYERS, b_p, RWKV_HEADS, RWKV_HEAD, RWKV_HEAD), f32)
    z_shift = jnp.zeros((N_RWKV_LAYERS, b_p, D_MODEL), x_prompt.dtype)
    y_prompt, s5_re_prompt, s5_im_prompt, rwkv_prompt, shift_prompt = trunk(
        x_prompt, mem_k_prompt, mem_v_prompt, z_s5, z_s5, z_rwkv, z_shift, p)
    y_sample, s5_re_sample, s5_im_sample, rwkv_sample, shift_sample = trunk(
        x_sample, cache_mem_k, cache_mem_v, state_s5_re, state_s5_im, state_rwkv, state_shift, p)
    return (y_prompt, y_sample, mem_k_prompt, mem_v_prompt, s5_re_prompt, s5_im_prompt,
            rwkv_prompt, shift_prompt, s5_re_sample, s5_im_sample, rwkv_sample, shift_sample)
```

```python
import functools

import jax
import jax.numpy as jnp
from jax import lax
from jax.experimental import pallas as pl
from jax.experimental.pallas import tpu as pltpu

F32 = jnp.float32
BF16 = jnp.bfloat16

D_MODEL = 1024
DEPTH = 4
S5_GROUP = 16
S5_GROUPS = 64
S5_STATE = 64
S5_WIDTH = S5_GROUPS * S5_STATE
S5_BLOCKS = 4
S5_BLOCK_IN = D_MODEL // S5_BLOCKS
S5_BLOCK_ST = S5_WIDTH // S5_BLOCKS
RWKV_HEADS = 16
RWKV_HEAD = 64
RWKV_PAIRS = RWKV_HEADS // 2
PAIR_W = 2 * RWKV_HEAD
RWKV_CHUNK = 64
RWKV_GN_EPS = 64e-5
MEM_TOKENS = 256
XA_HEADS = 4
XA_HEAD_DIM = 256
D_FF = 4 * D_MODEL
FF_CHUNK = 1024
LN_EPS = 1e-5
ALPHA = (2.0 * DEPTH) ** 0.25

V7X_VMEM_LIMIT = 56 * 1024 * 1024
ROW_TILE = 512


def _dot(a, b):
    return jnp.dot(a, b, preferred_element_type=F32)


def _dot_nt(a, b):
    return lax.dot_general(a, b, (((1,), (1,)), ((), ())), preferred_element_type=F32)


def _layer_norm(z, g, b):
    mu = jnp.mean(z, axis=-1, keepdims=True)
    zc = z - mu
    var = jnp.mean(zc * zc, axis=-1, keepdims=True)
    return zc * lax.rsqrt(var + LN_EPS) * g + b


def _const_spec(shape):
    nd = len(shape)
    return pl.BlockSpec(tuple(shape), lambda *_: (0,) * nd, pipeline_mode=pl.Buffered(1))


def _params(*sem):
    return pltpu.CompilerParams(dimension_semantics=sem, vmem_limit_bytes=V7X_VMEM_LIMIT)


def _rows_call(body, name, rows, tm, row_ins, const_ins, outs, *, scratch=(), sequential=False,
               const_outs=()):
    in_specs = [pl.BlockSpec((tm, a.shape[1]), lambda i: (i, 0)) for a in row_ins]
    in_specs += [_const_spec(a.shape) for a in const_ins]
    out_specs = [pl.BlockSpec((tm, c), lambda i: (i, 0)) for c, _ in outs]
    out_shape = [jax.ShapeDtypeStruct((rows, c), dt) for c, dt in outs]
    for shp, dt in const_outs:
        nd = len(shp)
        out_specs.append(pl.BlockSpec(tuple(shp), lambda i, nd=nd: (0,) * nd))
        out_shape.append(jax.ShapeDtypeStruct(tuple(shp), dt))
    return pl.pallas_call(
        body, grid=(rows // tm,), in_specs=in_specs, out_specs=out_specs, out_shape=out_shape,
        scratch_shapes=list(scratch),
        compiler_params=_params("arbitrary" if sequential else "parallel"), name=name,
    )(*row_ins, *const_ins)


def _matmul_body(x_ref, w_ref, o_ref):
    o_ref[...] = _dot(x_ref[...].astype(BF16), w_ref[...]).astype(o_ref.dtype)


def _matmul(x, w, name):
    rows = x.shape[0]
    return _rows_call(_matmul_body, name, rows, min(ROW_TILE, rows), [x], [w], [(w.shape[1], F32)])[0]


def _memkv_body(x_ref, w_ref, o_ref):
    o_ref[0] = _dot(x_ref[...].astype(BF16), w_ref[0].astype(BF16))


def _mem_kv(mem_rows, w_stack, name):
    rows = mem_rows.shape[0]
    return pl.pallas_call(
        _memkv_body, grid=(DEPTH,),
        in_specs=[_const_spec(mem_rows.shape), pl.BlockSpec((1, D_MODEL, D_MODEL), lambda l: (l, 0, 0))],
        out_specs=pl.BlockSpec((1, rows, D_MODEL), lambda l: (l, 0, 0)),
        out_shape=jax.ShapeDtypeStruct((DEPTH, rows, D_MODEL), F32),
        compiler_params=_params("parallel"), name=name,
    )(mem_rows, w_stack)


def _proj_ln_body(x_ref, o_ref, w_ref, g_ref, b_ref, out_ref):
    h = _dot(o_ref[...].astype(BF16), w_ref[...])
    out_ref[...] = _layer_norm(ALPHA * x_ref[...] + h, g_ref[...], b_ref[...])


def _mlp_body(x_ref, w1_ref, w2_ref, g_ref, b_ref, out_ref):
    x = x_ref[...]
    xb = x.astype(BF16)
    acc = None
    for c in range(D_FF // FF_CHUNK):
        h = _dot(xb, w1_ref[:, c * FF_CHUNK:(c + 1) * FF_CHUNK])
        h = jnp.maximum(h, 0.0)
        p = _dot((h * h).astype(BF16), w2_ref[c * FF_CHUNK:(c + 1) * FF_CHUNK, :])
        acc = p if acc is None else acc + p
    out_ref[...] = _layer_norm(ALPHA * x + acc, g_ref[...], b_ref[...])


def _attn_body(q_ref, k_ref, v_ref, o_ref):
    q = q_ref[...].astype(BF16)
    k = k_ref[...].astype(BF16)
    v = v_ref[...].astype(BF16)
    outs = []
    for h in range(XA_HEADS):
        sl = slice(h * XA_HEAD_DIM, (h + 1) * XA_HEAD_DIM)
        s = _dot_nt(q[:, sl], k[:, sl]) * (XA_HEAD_DIM ** -0.5)
        m = jnp.max(s, axis=-1, keepdims=True)
        e = jnp.exp(s - m)
        p = e / jnp.sum(e, axis=-1, keepdims=True)
        outs.append(_dot(p.astype(BF16), v[:, sl]))
    o_ref[...] = jnp.concatenate(outs, axis=-1)


def _attention(q_rows, mem_k, mem_v, layer, batch, t_len):
    tq = min(ROW_TILE, t_len)
    q2 = q_rows.reshape(t_len, batch * D_MODEL)
    kv_spec = pl.BlockSpec((None, None, MEM_TOKENS, D_MODEL), lambda b, t: (layer, b, 0, 0))
    o2 = pl.pallas_call(
        _attn_body, grid=(batch, t_len // tq),
        in_specs=[pl.BlockSpec((tq, D_MODEL), lambda b, t: (t, b)), kv_spec, kv_spec],
        out_specs=pl.BlockSpec((tq, D_MODEL), lambda b, t: (t, b)),
        out_shape=jax.ShapeDtypeStruct((t_len, batch * D_MODEL), F32),
        compiler_params=_params("parallel", "parallel"), name="xattn_core",
    )(q2, mem_k, mem_v)
    return o2.reshape(t_len * batch, D_MODEL)


def _s5_prep_body(lre_ref, lim_ref, ldt_ref, br_ref, bi_ref, are_ref, aim_ref, bbre_ref, bbim_ref):
    lre = lre_ref[...]
    lim = lim_ref[...]
    dt = jnp.exp(ldt_ref[...])
    mag = jnp.exp(lre * dt)
    ang = lim * dt
    ab_re = mag * jnp.cos(ang)
    ab_im = mag * jnp.sin(ang)
    den = lre * lre + lim * lim
    num_re = ab_re - 1.0
    f_re = (num_re * lre + ab_im * lim) / den
    f_im = (ab_im * lre - num_re * lim) / den
    br = br_ref[...]
    bi = bi_ref[...]
    are_ref[...] = ab_re
    aim_ref[...] = ab_im
    bbre_ref[...] = f_re * br - f_im * bi
    bbim_ref[...] = f_re * bi + f_im * br


def _s5_prep(a_re, a_im, log_dt, b_re, b_im, c_re, c_im):
    rep = lambda t: jnp.repeat(t, S5_GROUP, axis=0)
    to_rows = lambda t: jnp.transpose(t, (0, 2, 1)).reshape(S5_GROUPS * S5_GROUP, S5_STATE)
    ins = [rep(a_re), rep(a_im), rep(jnp.broadcast_to(log_dt[:, None], (S5_GROUPS, S5_STATE))),
           to_rows(b_re), to_rows(b_im)]
    shp = jax.ShapeDtypeStruct((S5_GROUPS * S5_GROUP, S5_STATE), F32)
    full = pl.BlockSpec((S5_GROUPS * S5_GROUP, S5_STATE), lambda: (0, 0))
    are, aim, bbre, bbim = pl.pallas_call(
        _s5_prep_body, in_specs=[full] * 5, out_specs=[full] * 4, out_shape=[shp] * 4, name="s5_prep",
    )(*ins)
    a_row = lambda t: t[::S5_GROUP].reshape(1, S5_WIDTH)
    eye = jnp.eye(S5_GROUP, dtype=F32)
    blk_b = lambda t: jnp.einsum('qgcn,gh->qgchn', t.reshape(S5_BLOCKS, 16, S5_GROUP, S5_STATE), eye
                                 ).reshape(S5_BLOCKS, S5_BLOCK_IN, S5_BLOCK_ST)
    wb = jnp.concatenate([blk_b(bbre), blk_b(bbim)], axis=2).astype(BF16)
    blk_c = lambda t: jnp.einsum('qgcn,gh->qhngc', t.reshape(S5_BLOCKS, 16, S5_GROUP, S5_STATE), eye
                                 ).reshape(S5_BLOCKS, S5_BLOCK_ST, S5_BLOCK_IN)
    wc = jnp.concatenate([blk_c(c_re), -blk_c(c_im)], axis=1).astype(BF16)
    return a_row(are), a_row(aim), wb, wc


def _s5_body(batch, x_ref, h0re_ref, h0im_ref, are_ref, aim_ref, wb_ref, wc_ref, d_ref, wv_ref, wg_ref,
             g_ref, b_ref, out_ref, hre_ref, him_ref, hbuf, sre, sim):
    rows = x_ref.shape[0]
    steps = rows // batch

    @pl.when(pl.program_id(0) == 0)
    def _():
        sre[...] = h0re_ref[...]
        sim[...] = h0im_ref[...]

    x = x_ref[...]
    xb = x.astype(BF16)
    for q in range(S5_BLOCKS):
        hbuf[:, q * 2 * S5_BLOCK_ST:(q + 1) * 2 * S5_BLOCK_ST] = _dot(
            xb[:, q * S5_BLOCK_IN:(q + 1) * S5_BLOCK_IN], wb_ref[q])

    def row_group(rg, carry):
        r_off = pl.multiple_of(rg * 8, 8)
        for q in range(S5_BLOCKS):
            st = slice(q * S5_BLOCK_ST, (q + 1) * S5_BLOCK_ST)
            c_re = slice(q * 2 * S5_BLOCK_ST, q * 2 * S5_BLOCK_ST + S5_BLOCK_ST)
            c_im = slice(q * 2 * S5_BLOCK_ST + S5_BLOCK_ST, (q + 1) * 2 * S5_BLOCK_ST)
            ar = jnp.broadcast_to(are_ref[:, st], (8, S5_BLOCK_ST))
            ai = jnp.broadcast_to(aim_ref[:, st], (8, S5_BLOCK_ST))

            def step(t, h):
                hr, hi = h
                row = pl.multiple_of(t * batch + r_off, 8)
                nr = ar * hr - ai * hi + hbuf[pl.ds(row, 8), c_re]
                ni = ar * hi + ai * hr + hbuf[pl.ds(row, 8), c_im]
                hbuf[pl.ds(row, 8), c_re] = nr
                hbuf[pl.ds(row, 8), c_im] = ni
                return nr, ni

            hr, hi = lax.fori_loop(0, steps, step, (sre[pl.ds(r_off, 8), st], sim[pl.ds(r_off, 8), st]))
            sre[pl.ds(r_off, 8), st] = hr
            sim[pl.ds(r_off, 8), st] = hi
        return carry

    lax.fori_loop(0, batch // 8, row_group, 0)
    hre_ref[...] = sre[...]
    him_ref[...] = sim[...]

    ys = [_dot(hbuf[:, q * 2 * S5_BLOCK_ST:(q + 1) * 2 * S5_BLOCK_ST].astype(BF16), wc_ref[q])
          for q in range(S5_BLOCKS)]
    y = jnp.concatenate(ys, axis=-1) + d_ref[...] * x
    z = jax.nn.gelu(y).astype(BF16)
    h = _dot(z, wv_ref[...]) * jax.nn.sigmoid(_dot(z, wg_ref[...]))
    out_ref[...] = _layer_norm(ALPHA * x + h, g_ref[...], b_ref[...])


def _s5_layer(x, batch, h0re, h0im, s5w, ln_g, ln_b):
    rows = x.shape[0]
    tm = min(ROW_TILE, rows)
    a_re, a_im, wb, wc, d_vec, wv, wg = s5w
    out, hre, him = _rows_call(
        functools.partial(_s5_body, batch), "s5_mixer", rows, tm, [x],
        [h0re, h0im, a_re, a_im, wb, wc, d_vec, wv, wg, ln_g, ln_b], [(D_MODEL, F32)],
        scratch=[pltpu.VMEM((tm, 2 * S5_WIDTH), F32), pltpu.VMEM((batch, S5_WIDTH), F32),
                 pltpu.VMEM((batch, S5_WIDTH), F32)],
        sequential=True, const_outs=[((batch, S5_WIDTH), F32)] * 2)
    return out, hre, him


def _head_sum(v, ones_ref):
    return _dot(v.astype(BF16), ones_ref[...])


def _rwkv_proj_body(batch, has_vgate, *refs):
    if has_vgate:
        (x_ref, vf_ref, shift_ref, mu_ref, wr_ref, wk_ref, wv_ref, w1_ref, w2_ref, a1_ref, a2_ref, g1_ref,
         g2_ref, v1_ref, v2_ref, vec_ref, ones_ref,
         r_out, lw_out, k_out, v_out, a_out, b_out, g_out, xprev) = refs
    else:
        (x_ref, shift_ref, mu_ref, wr_ref, wk_ref, wv_ref, w1_ref, w2_ref, a1_ref, a2_ref, g1_ref,
         g2_ref, vec_ref, ones_ref,
         r_out, lw_out, k_out, v_out, a_out, b_out, g_out, xprev) = refs
    rows = x_ref.shape[0]

    @pl.when(pl.program_id(0) == 0)
    def _():
        xprev[...] = shift_ref[...]

    x = x_ref[...]
    if rows > batch:
        xs = jnp.concatenate([xprev[...], x[:rows - batch]], axis=0)
    else:
        xs = xprev[...]
    xprev[...] = x[rows - batch:]
    xx = xs - x
    mix = lambda i: (x + xx * mu_ref[i:i + 1, :]).astype(BF16)
    w0, a0, v0, k_k, k_a = (vec_ref[i:i + 1, :] for i in range(5))

    r = _dot(mix(0), wr_ref[...])
    k = _dot(mix(2), wk_ref[...])
    xv = mix(3)
    v = _dot(xv, wv_ref[...])
    w_lora = _dot(jnp.tanh(_dot(mix(1), w1_ref[...])).astype(BF16), w2_ref[...])
    w_log = -jax.nn.softplus(-(w0 + w_lora)) - 0.5
    lw_out[...] = -jnp.exp(w_log)
    if has_vgate:
        gate_v = jax.nn.sigmoid(v0 + _dot(_dot(xv, v1_ref[...]).astype(BF16), v2_ref[...]))
        v = v + (vf_ref[...] - v) * gate_v
    a = jax.nn.sigmoid(a0 + _dot(_dot(mix(4), a1_ref[...]).astype(BF16), a2_ref[...]))
    g_out[...] = _dot(jax.nn.sigmoid(_dot(mix(5), g1_ref[...])).astype(BF16), g2_ref[...])
    kk = k * k_k
    sq = kk * kk
    sq_hi = sq.astype(BF16)
    ss = _dot(sq_hi, ones_ref[...]) + _head_sum(sq - sq_hi.astype(F32), ones_ref)
    kk = kk / jnp.maximum(jnp.sqrt(ss), 1e-12)
    r_out[...] = r
    k_out[...] = k * (1.0 + (a - 1.0) * k_a)
    v_out[...] = v
    a_out[...] = -kk
    b_out[...] = kk * a


def _rwkv_scan_body(n_sub, r_ref, lw_ref, k_ref, v_ref, a_ref, b_ref, s0_ref, y_ref, sout_ref, s_scr):
    L = RWKV_CHUNK
    rows_in = r_ref.shape[0]
    lane = lax.broadcasted_iota(jnp.int32, (1, PAIR_W), 1)
    m0 = lane < RWKV_HEAD
    t_idx = lax.broadcasted_iota(jnp.int32, (L, PAIR_W), 0)
    s_idx = lax.broadcasted_iota(jnp.int32, (L, PAIR_W), 1) % L
    strict_lower = s_idx < t_idx
    incl_lower = s_idx <= t_idx
    r2 = lax.broadcasted_iota(jnp.int32, (2 * L, PAIR_W), 0)
    c2 = lax.broadcasted_iota(jnp.int32, (2 * L, PAIR_W), 1)
    same_head = (r2 < L) == (c2 < RWKV_HEAD)
    eye2 = jnp.where(r2 == c2, 1.0, 0.0).astype(F32)
    tri = jnp.where(lax.broadcasted_iota(jnp.int32, (L, L), 1) <= lax.broadcasted_iota(jnp.int32, (L, L), 0),
                    1.0, 0.0).astype(BF16)
    zeros_l = jnp.zeros((L, PAIR_W), F32)

    @pl.when(pl.program_id(1) == 0)
    def _():
        for p in range(RWKV_PAIRS):
            sc = s0_ref[p]
            s_scr[p] = jnp.concatenate([jnp.where(m0, sc, 0.0), jnp.where(m0, 0.0, sc)], axis=0)

    def load(ref, c):
        if rows_in < L:
            return jnp.concatenate([ref[...], jnp.zeros((L - rows_in, D_MODEL), F32)], axis=0)
        return ref[pl.ds(pl.multiple_of(c * L, L), L), :]

    def chunk(c, carry):
        r, lw, k, v, a, b = (load(ref, c) for ref in (r_ref, lw_ref, k_ref, v_ref, a_ref, b_ref))
        h1 = lw.astype(BF16)
        e1 = lw - h1.astype(F32)
        h2 = e1.astype(BF16)
        h3 = (e1 - h2.astype(F32)).astype(BF16)
        cum = _dot(tri, h1) + _dot(tri, h2) + _dot(tri, h3)
        tot = cum[L - 1:L, :]
        dec = jnp.exp(cum)
        inv = jnp.exp(-cum)
        tail = jnp.exp(tot - cum)
        dec_tot = jnp.exp(tot)
        a_t = a * jnp.exp(cum - lw)
        r_t = r * dec
        b_t = b * inv
        k_t = k * inv
        wb = b * tail
        wk = k * tail
        for p in range(RWKV_PAIRS):
            sl = slice(p * PAIR_W, (p + 1) * PAIR_W)
            ap, rp, vp = a_t[:, sl], r_t[:, sl], v[:, sl]
            lhs = jnp.concatenate([jnp.where(m0, ap, 0.0), jnp.where(m0, 0.0, ap),
                                   jnp.where(m0, rp, 0.0), jnp.where(m0, 0.0, rp)], axis=0).astype(BF16)
            rhs = jnp.concatenate([b_t[:, sl], k_t[:, sl]], axis=0).astype(BF16)
            gram = _dot_nt(lhs, rhs)
            na0 = jnp.where(strict_lower, gram[0:L], 0.0)
            na1 = jnp.where(strict_lower, gram[L:2 * L], 0.0)
            nr = jnp.concatenate([jnp.where(incl_lower, gram[2 * L:3 * L], 0.0),
                                  jnp.where(incl_lower, gram[3 * L:4 * L], 0.0)], axis=0)
            n_bd = jnp.concatenate([jnp.where(m0, na0, 0.0),
                                    jnp.where(m0, 0.0, pltpu.roll(na1, RWKV_HEAD, axis=1))], axis=0)
            t_inv = eye2 + n_bd
            pw = n_bd
            for _ in range(5):
                pwb = pw.astype(BF16)
                pw = _dot(pwb, pwb)
                t_inv = t_inv + _dot(t_inv.astype(BF16), pw.astype(BF16))
            s_bd = s_scr[p]
            ar_s = _dot_nt(jnp.concatenate([ap, rp], axis=0).astype(BF16), s_bd.astype(BF16))
            nv = _dot(jnp.concatenate([na0, na1], axis=0).astype(BF16),
                      jnp.concatenate([zeros_l, vp], axis=0).astype(BF16))
            z_st = jnp.concatenate([ar_s[0:L], ar_s[0:L]], axis=0) + nv
            u_st = _dot(t_inv.astype(BF16), z_st.astype(BF16))
            u = jnp.where(m0, u_st[0:L], u_st[L:2 * L])
            zz = jnp.concatenate([u, vp], axis=0)
            y_st = _dot(nr.astype(BF16), zz.astype(BF16))
            y = ar_s[L:2 * L] + jnp.where(m0, y_st[0:L], y_st[L:2 * L])
            if rows_in < L:
                y_ref[:, sl] = y[0:rows_in]
            else:
                y_ref[pl.ds(pl.multiple_of(c * L, L), L), sl] = y
            w_st = jnp.concatenate([wb[:, sl], wk[:, sl]], axis=0)
            upd = _dot(zz.T.astype(BF16), w_st.astype(BF16))
            s_scr[p] = s_bd * dec_tot[:, sl] + jnp.where(same_head, upd, 0.0)
        return carry

    lax.fori_loop(0, n_sub, chunk, 0)

    @pl.when(pl.program_id(1) == pl.num_programs(1) - 1)
    def _():
        for p in range(RWKV_PAIRS):
            s_bd = s_scr[p]
            sout_ref[p] = s_bd[0:RWKV_HEAD] + s_bd[RWKV_HEAD:PAIR_W]


def _rwkv_out_body(y_ref, r_ref, k_ref, v_ref, g_ref, x_ref, ones_ref, vec_ref, wo_ref, lg_ref, lb_ref, out_ref):
    y = y_ref[...]
    r_k, lnx_g, lnx_b = (vec_ref[i:i + 1, :] for i in range(3))
    inv_n = 1.0 / RWKV_HEAD
    mu = _head_sum(y, ones_ref) * inv_n
    yc = y - mu
    var = _head_sum(yc * yc, ones_ref) * inv_n
    yn = yc * lax.rsqrt(var + RWKV_GN_EPS) * lnx_g + lnx_b
    bonus = _head_sum(r_ref[...] * k_ref[...] * r_k, ones_ref) * v_ref[...]
    o = ((yn + bonus) * g_ref[...]).astype(BF16)
    out_ref[...] = _layer_norm(ALPHA * x_ref[...] + _dot(o, wo_ref[...]), lg_ref[...], lb_ref[...])


def _pair_state(s):
    bsz = s.shape[0]
    s = s.reshape(bsz, RWKV_PAIRS, 2, RWKV_HEAD, RWKV_HEAD)
    return jnp.transpose(s, (0, 1, 3, 2, 4)).reshape(bsz, RWKV_PAIRS, RWKV_HEAD, PAIR_W)


def _unpair_state(s):
    bsz = s.shape[0]
    s = s.reshape(bsz, RWKV_PAIRS, RWKV_HEAD, 2, RWKV_HEAD)
    return jnp.transpose(s, (0, 1, 3, 2, 4)).reshape(bsz, RWKV_HEADS, RWKV_HEAD, RWKV_HEAD)


def _rwkv_layer(x, batch, t_len, s0, shift0, v_first, rw, ln_g, ln_b):
    rows = x.shape[0]
    tm = min(ROW_TILE, rows)
    has_vgate = v_first is not None
    row_ins = [x] + ([v_first] if has_vgate else [])
    consts = [shift0, rw["mu"], rw["w_r"], rw["w_k"], rw["w_v"], rw["w1"], rw["w2"], rw["a1"], rw["a2"],
              rw["g1"], rw["g2"]]
    if has_vgate:
        consts += [rw["v1"], rw["v2"]]
    consts += [rw["proj_vec"], rw["ones"]]
    r, lw, k, v, a, b, g = _rows_call(
        functools.partial(_rwkv_proj_body, batch, has_vgate), "rwkv_proj", rows, tm // 2, row_ins, consts,
        [(D_MODEL, F32)] * 7, scratch=[pltpu.VMEM((batch, D_MODEL), F32)], sequential=True)

    tc = min(4 * RWKV_CHUNK, t_len)
    n_sub = max(tc // RWKV_CHUNK, 1)
    view = lambda t: t.reshape(t_len, batch * D_MODEL)
    tile = pl.BlockSpec((tc, D_MODEL), lambda bi, ti: (ti, bi))
    st_spec = pl.BlockSpec((None, RWKV_PAIRS, RWKV_HEAD, PAIR_W), lambda bi, ti: (bi, 0, 0, 0))
    y2, s_last = pl.pallas_call(
        functools.partial(_rwkv_scan_body, n_sub), grid=(batch, t_len // tc),
        in_specs=[tile] * 6 + [st_spec], out_specs=[tile, st_spec],
        out_shape=[jax.ShapeDtypeStruct((t_len, batch * D_MODEL), F32),
                   jax.ShapeDtypeStruct((batch, RWKV_PAIRS, RWKV_HEAD, PAIR_W), F32)],
        scratch_shapes=[pltpu.VMEM((RWKV_PAIRS, PAIR_W, PAIR_W), F32)],
        compiler_params=_params("arbitrary", "arbitrary"), name="rwkv_scan",
    )(view(r), view(lw), view(k), view(v), view(a), view(b), _pair_state(s0))
    y = y2.reshape(rows, D_MODEL)

    out = _rows_call(_rwkv_out_body, "rwkv_out", rows, tm // 2, [y, r, k, v, g, x],
                     [rw["ones"], rw["out_vec"], rw["w_o"], ln_g, ln_b], [(D_MODEL, F32)])[0]
    return out, _unpair_state(s_last), x[rows - batch:], (v if not has_vgate else v_first)


def _trunk(x, batch, t_len, mem_k, mem_v, s5_re, s5_im, rwkv_s, shift, w):
    rows = x.shape[0]
    tm = min(ROW_TILE, rows)
    new_re, new_im, new_s, new_shift = [], [], [], []
    v_first = None
    for i in range(DEPTH):
        j = i // 2
        ln = lambda n: (w["ln_g"][i, n][None, :], w["ln_b"][i, n][None, :])
        if i % 2 == 0:
            x, hre, him = _s5_layer(x, batch, s5_re[j].reshape(batch, S5_WIDTH),
                                    s5_im[j].reshape(batch, S5_WIDTH), w["s5"][j], *ln(0))
            new_re.append(hre.reshape(batch, S5_GROUPS, S5_STATE))
            new_im.append(him.reshape(batch, S5_GROUPS, S5_STATE))
        else:
            x, s_last, x_last, v_first = _rwkv_layer(x, batch, t_len, rwkv_s[j], shift[j], v_first,
                                                     w["rwkv"][j], *ln(0))
            new_s.append(s_last)
            new_shift.append(x_last)
        q = _matmul(x, w["xa_q"][i], "xattn_q")
        o = _attention(q, mem_k, mem_v, i, batch, t_len)
        x = _rows_call(_proj_ln_body, "xattn_out", rows, tm, [x, o], [w["xa_o"][i], *ln(1)], [(D_MODEL, F32)])[0]
        x = _rows_call(_mlp_body, "mlp", rows, tm, [x], [w["mlp1"][i], w["mlp2"][i], *ln(2)], [(D_MODEL, F32)])[0]
    return x, jnp.stack(new_re), jnp.stack(new_im), jnp.stack(new_s), jnp.stack(new_shift)


def _time_major(x):
    bsz, t_len, d = x.shape
    return jnp.transpose(x, (1, 0, 2)).reshape(t_len * bsz, d)


def _batch_major(x, bsz, t_len):
    return jnp.transpose(x.reshape(t_len, bsz, -1), (1, 0, 2))


def kernel(x_prompt, x_sample, mem_prompt, cache_mem_k, cache_mem_v, state_s5_re, state_s5_im, state_rwkv, state_shift, ln_g, ln_b, s5_a_re, s5_a_im, s5_log_dt, s5_b_re, s5_b_im, s5_c_re, s5_c_im, s5_d, s5_w_glu_v, s5_w_glu_g, rwkv_mu, rwkv_w_r, rwkv_w_k, rwkv_w_v, rwkv_w_o, rwkv_w0, rwkv_w1, rwkv_w2, rwkv_a0, rwkv_a1, rwkv_a2, rwkv_v0, rwkv_v1, rwkv_v2, rwkv_g1, rwkv_g2, rwkv_k_k, rwkv_k_a, rwkv_r_k, rwkv_lnx_g, rwkv_lnx_b, xa_w_q, xa_w_k, xa_w_v, xa_w_o, mlp_w1, mlp_w2):
    bf = lambda t: t.astype(BF16)
    n_s5, n_rwkv = state_s5_re.shape[0], state_rwkv.shape[0]
    b_p, t_p = x_prompt.shape[0], x_prompt.shape[1]
    b_s, t_s = x_sample.shape[0], x_sample.shape[1]

    head_id = jnp.arange(D_MODEL) // RWKV_HEAD
    ones_bd = (head_id[:, None] == head_id[None, :]).astype(BF16)
    w = {"ln_g": ln_g, "ln_b": ln_b, "s5": [], "rwkv": [],
         "xa_q": [bf(xa_w_q[i]) for i in range(DEPTH)], "xa_o": [bf(xa_w_o[i]) for i in range(DEPTH)],
         "mlp1": [bf(mlp_w1[i]) for i in range(DEPTH)], "mlp2": [bf(mlp_w2[i]) for i in range(DEPTH)]}
    for j in range(n_s5):
        a_re, a_im, wb, wc = _s5_prep(s5_a_re[j], s5_a_im[j], s5_log_dt[j], s5_b_re[j], s5_b_im[j],
                                      s5_c_re[j], s5_c_im[j])
        w["s5"].append((a_re, a_im, wb, wc, s5_d[j][None, :], bf(s5_w_glu_v[j]), bf(s5_w_glu_g[j])))
    for j in range(n_rwkv):
        v0 = rwkv_v0[j - 1] if j > 0 else jnp.zeros((D_MODEL,), F32)
        rw = {"mu": rwkv_mu[j], "w_r": bf(rwkv_w_r[j]), "w_k": bf(rwkv_w_k[j]), "w_v": bf(rwkv_w_v[j]),
              "w_o": bf(rwkv_w_o[j]), "w1": bf(rwkv_w1[j]), "w2": bf(rwkv_w2[j]), "a1": bf(rwkv_a1[j]),
              "a2": bf(rwkv_a2[j]), "g1": bf(rwkv_g1[j]), "g2": bf(rwkv_g2[j]), "ones": ones_bd,
              "proj_vec": jnp.stack([rwkv_w0[j], rwkv_a0[j], v0, rwkv_k_k[j], rwkv_k_a[j]]),
              "out_vec": jnp.stack([rwkv_r_k[j].reshape(D_MODEL), rwkv_lnx_g[j], rwkv_lnx_b[j]])}
        if j > 0:
            rw["v1"], rw["v2"] = bf(rwkv_v1[j - 1]), bf(rwkv_v2[j - 1])
        w["rwkv"].append(rw)

    mem_rows = mem_prompt.reshape(b_p * MEM_TOKENS, D_MODEL)
    mk = _mem_kv(mem_rows, xa_w_k, "mem_k").reshape(DEPTH, b_p, MEM_TOKENS, D_MODEL)
    mv = _mem_kv(mem_rows, xa_w_v, "mem_v").reshape(DEPTH, b_p, MEM_TOKENS, D_MODEL)
    z_s5 = jnp.zeros((n_s5, b_p, S5_GROUPS, S5_STATE), F32)
    z_rwkv = jnp.zeros((n_rwkv, b_p, RWKV_HEADS, RWKV_HEAD, RWKV_HEAD), F32)
    z_shift = jnp.zeros((n_rwkv, b_p, D_MODEL), F32)
    y_p, re_p, im_p, s_p, sh_p = _trunk(_time_major(x_prompt), b_p, t_p, mk, mv, z_s5, z_s5, z_rwkv, z_shift, w)

    ck = cache_mem_k.reshape(DEPTH, b_s, MEM_TOKENS, D_MODEL)
    cv = cache_mem_v.reshape(DEPTH, b_s, MEM_TOKENS, D_MODEL)
    y_s, re_s, im_s, s_s, sh_s = _trunk(_time_major(x_sample), b_s, t_s, ck, cv, state_s5_re, state_s5_im,
                                        state_rwkv, state_shift, w)

    kv_shape = (DEPTH, b_p, MEM_TOKENS, XA_HEADS, XA_HEAD_DIM)
    return (_batch_major(y_p, b_p, t_p), _batch_major(y_s, b_s, t_s), mk.reshape(kv_shape), mv.reshape(kv_shape),
            re_p, im_p, s_p, sh_p, re_s, im_s, s_s, sh_s)
```

```python
import functools

import jax
import jax.numpy as jnp
from jax import lax
from jax.experimental import pallas as pl
from jax.experimental.pallas import tpu as pltpu

F32 = jnp.float32
BF16 = jnp.bfloat16

D_MODEL = 1024
DEPTH = 4
S5_GROUP = 16
S5_GROUPS = 64
S5_STATE = 64
S5_WIDTH = S5_GROUPS * S5_STATE
S5_BLOCKS = 4
S5_BLOCK_IN = D_MODEL // S5_BLOCKS
S5_BLOCK_ST = S5_WIDTH // S5_BLOCKS
RWKV_HEADS = 16
RWKV_HEAD = 64
RWKV_PAIRS = RWKV_HEADS // 2
PAIR_W = 2 * RWKV_HEAD
RWKV_CHUNK = 64
RWKV_GN_EPS = 64e-5
SCAN_SEQS = 8
SCAN_LANES = 512
SCAN_PAIRS = SCAN_LANES // PAIR_W
MEM_TOKENS = 256
XA_HEADS = 4
XA_HEAD_DIM = 256
D_FF = 4 * D_MODEL
FF_CHUNK = 1024
LN_EPS = 1e-5
ALPHA = (2.0 * DEPTH) ** 0.25

V7X_VMEM_LIMIT = 56 * 1024 * 1024
ROW_TILE = 512
XA_ROWS = 1024


def _dot(a, b):
    return jnp.dot(a, b, preferred_element_type=F32)


def _dot_nt(a, b):
    return lax.dot_general(a, b, (((1,), (1,)), ((), ())), preferred_element_type=F32)


def _layer_norm(z, g, b):
    mu = jnp.mean(z, axis=-1, keepdims=True)
    zc = z - mu
    var = jnp.mean(zc * zc, axis=-1, keepdims=True)
    return zc * lax.rsqrt(var + LN_EPS) * g + b


def _to_seq_major(x):
    return jnp.transpose(x.reshape(x.shape[0] // 8, 8, x.shape[1]), (1, 0, 2))


def _to_time_major(x3):
    return jnp.transpose(x3, (1, 0, 2)).reshape(x3.shape[0] * x3.shape[1], x3.shape[2])


def _const_spec(shape):
    nd = len(shape)
    return pl.BlockSpec(tuple(shape), lambda *_: (0,) * nd, pipeline_mode=pl.Buffered(1))


def _params(*sem):
    return pltpu.CompilerParams(dimension_semantics=sem, vmem_limit_bytes=V7X_VMEM_LIMIT)


def _rows_call(body, name, rows, tm, row_ins, const_ins, outs, *, scratch=(), sequential=False,
               const_outs=()):
    in_specs = [pl.BlockSpec((tm, a.shape[1]), lambda i: (i, 0)) for a in row_ins]
    in_specs += [_const_spec(a.shape) for a in const_ins]
    out_specs = [pl.BlockSpec((tm, c), lambda i: (i, 0)) for c, _ in outs]
    out_shape = [jax.ShapeDtypeStruct((rows, c), dt) for c, dt in outs]
    for shp, dt in const_outs:
        nd = len(shp)
        out_specs.append(pl.BlockSpec(tuple(shp), lambda i, nd=nd: (0,) * nd))
        out_shape.append(jax.ShapeDtypeStruct(tuple(shp), dt))
    return pl.pallas_call(
        body, grid=(rows // tm,), in_specs=in_specs, out_specs=out_specs, out_shape=out_shape,
        scratch_shapes=list(scratch),
        compiler_params=_params("arbitrary" if sequential else "parallel"), name=name,
    )(*row_ins, *const_ins)


def _matmul_body(x_ref, w_ref, o_ref):
    o_ref[...] = _dot(x_ref[...].astype(BF16), w_ref[...]).astype(o_ref.dtype)


def _matmul(x, w, name):
    rows = x.shape[0]
    return _rows_call(_matmul_body, name, rows, min(ROW_TILE, rows), [x], [w], [(w.shape[1], F32)])[0]


def _memkv_body(x_ref, w_ref, o_ref):
    o_ref[0] = _dot(x_ref[...].astype(BF16), w_ref[0].astype(BF16))


def _mem_kv(mem_rows, w_stack, name):
    rows = mem_rows.shape[0]
    return pl.pallas_call(
        _memkv_body, grid=(DEPTH,),
        in_specs=[_const_spec(mem_rows.shape), pl.BlockSpec((1, D_MODEL, D_MODEL), lambda l: (l, 0, 0))],
        out_specs=pl.BlockSpec((1, rows, D_MODEL), lambda l: (l, 0, 0)),
        out_shape=jax.ShapeDtypeStruct((DEPTH, rows, D_MODEL), F32),
        compiler_params=_params("parallel"), name=name,
    )(mem_rows, w_stack)


def _proj_ln_body(x_ref, o_ref, w_ref, g_ref, b_ref, out_ref):
    h = _dot(o_ref[...].astype(BF16), w_ref[...])
    out_ref[...] = _layer_norm(ALPHA * x_ref[...] + h, g_ref[...], b_ref[...])


def _mlp_body(x_ref, w1_ref, w2_ref, g_ref, b_ref, out_ref):
    x = x_ref[...]
    xb = x.astype(BF16)
    acc = None
    for c in range(D_FF // FF_CHUNK):
        h = _dot(xb, w1_ref[:, c * FF_CHUNK:(c + 1) * FF_CHUNK])
        h = jnp.maximum(h, 0.0)
        p = _dot((h * h).astype(BF16), w2_ref[c * FF_CHUNK:(c + 1) * FF_CHUNK, :])
        acc = p if acc is None else acc + p
    out_ref[...] = _layer_norm(ALPHA * x + acc, g_ref[...], b_ref[...])


def _softmax_rows(s):
    m = jnp.max(s, axis=-1, keepdims=True)
    e = jnp.exp(s - m)
    return e / jnp.sum(e, axis=-1, keepdims=True)


def _xattn_fused_body(x_ref, wq_ref, k_ref, v_ref, wo_ref, g_ref, b_ref, out_ref):
    x = x_ref[...]
    q3 = _to_seq_major(_dot(x.astype(BF16), wq_ref[...])).astype(BF16)
    heads = [slice(h * XA_HEAD_DIM, (h + 1) * XA_HEAD_DIM) for h in range(XA_HEADS)]
    per_seq = []
    for b in range(SCAN_SEQS):
        qb, kb, vb = q3[b], k_ref[b], v_ref[b]
        s = [_dot_nt(qb[:, sl], kb[:, sl]) * (XA_HEAD_DIM ** -0.5) for sl in heads]
        p = [_softmax_rows(t).astype(BF16) for t in s]
        per_seq.append(jnp.concatenate([_dot(p[h], vb[:, heads[h]]) for h in range(XA_HEADS)], axis=-1))
    o = _to_time_major(jnp.stack(per_seq, axis=0))
    out_ref[...] = _layer_norm(ALPHA * x + _dot(o.astype(BF16), wo_ref[...]), g_ref[...], b_ref[...])


def _xattn_fused(x, wq, mem_k, mem_v, wo, ln_g, ln_b, layer):
    rows = x.shape[0]
    tm = min(XA_ROWS, rows)
    tile = pl.BlockSpec((tm, D_MODEL), lambda i: (i, 0))
    kv_spec = pl.BlockSpec((None, SCAN_SEQS, MEM_TOKENS, D_MODEL), lambda i: (layer, 0, 0, 0),
                           pipeline_mode=pl.Buffered(1))
    vec = _const_spec((1, D_MODEL))
    return pl.pallas_call(
        _xattn_fused_body, grid=(rows // tm,),
        in_specs=[tile, _const_spec(wq.shape), kv_spec, kv_spec, _const_spec(wo.shape), vec, vec],
        out_specs=tile, out_shape=jax.ShapeDtypeStruct((rows, D_MODEL), F32),
        compiler_params=_params("parallel"), name="xattn_fused",
    )(x, wq, mem_k, mem_v, wo, ln_g, ln_b)


def _attn_cache_body(q_ref, k_ref, v_ref, o_ref):
    t_len = q_ref.shape[0]
    q = q_ref[...]
    qs = jnp.concatenate([q[:, h * XA_HEAD_DIM:(h + 1) * XA_HEAD_DIM] for h in range(XA_HEADS)], axis=0)
    k2 = k_ref[...].reshape(MEM_TOKENS * XA_HEADS, XA_HEAD_DIM).astype(BF16)
    v2 = v_ref[...].reshape(MEM_TOKENS * XA_HEADS, XA_HEAD_DIM).astype(BF16)
    s = _dot_nt(qs.astype(BF16), k2) * (XA_HEAD_DIM ** -0.5)
    shape = (XA_HEADS * t_len, MEM_TOKENS * XA_HEADS)
    own_head = (lax.broadcasted_iota(jnp.int32, shape, 0) // t_len
                == lax.broadcasted_iota(jnp.int32, shape, 1) % XA_HEADS)
    p = _softmax_rows(jnp.where(own_head, s, -1e30))
    o = _dot(p.astype(BF16), v2)
    o_ref[...] = jnp.concatenate([o[h * t_len:(h + 1) * t_len] for h in range(XA_HEADS)], axis=-1)


def _attention_cache(q2, cache_k, cache_v, layer, batch, t_len):
    kv_spec = pl.BlockSpec((None, None, MEM_TOKENS, XA_HEADS, XA_HEAD_DIM), lambda b: (layer, b, 0, 0, 0))
    tile = pl.BlockSpec((t_len, D_MODEL), lambda b: (0, b))
    return pl.pallas_call(
        _attn_cache_body, grid=(batch,), in_specs=[tile, kv_spec, kv_spec], out_specs=tile,
        out_shape=jax.ShapeDtypeStruct((t_len, batch * D_MODEL), F32),
        compiler_params=_params("parallel"), name="xattn_cache",
    )(q2, cache_k, cache_v)


def _s5_prep_body(lre_ref, lim_ref, ldt_ref, br_ref, bi_ref, are_ref, aim_ref, bbre_ref, bbim_ref):
    lre = lre_ref[...]
    lim = lim_ref[...]
    dt = jnp.exp(ldt_ref[...])
    mag = jnp.exp(lre * dt)
    ang = lim * dt
    ab_re = mag * jnp.cos(ang)
    ab_im = mag * jnp.sin(ang)
    den = lre * lre + lim * lim
    num_re = ab_re - 1.0
    f_re = (num_re * lre + ab_im * lim) / den
    f_im = (ab_im * lre - num_re * lim) / den
    br = br_ref[...]
    bi = bi_ref[...]
    are_ref[...] = ab_re
    aim_ref[...] = ab_im
    bbre_ref[...] = f_re * br - f_im * bi
    bbim_ref[...] = f_re * bi + f_im * br


def _s5_prep(a_re, a_im, log_dt, b_re, b_im, c_re, c_im):
    rep = lambda t: jnp.repeat(t, S5_GROUP, axis=0)
    to_rows = lambda t: jnp.transpose(t, (0, 2, 1)).reshape(S5_GROUPS * S5_GROUP, S5_STATE)
    ins = [rep(a_re), rep(a_im), rep(jnp.broadcast_to(log_dt[:, None], (S5_GROUPS, S5_STATE))),
           to_rows(b_re), to_rows(b_im)]
    shp = jax.ShapeDtypeStruct((S5_GROUPS * S5_GROUP, S5_STATE), F32)
    full = pl.BlockSpec((S5_GROUPS * S5_GROUP, S5_STATE), lambda: (0, 0))
    are, aim, bbre, bbim = pl.pallas_call(
        _s5_prep_body, in_specs=[full] * 5, out_specs=[full] * 4, out_shape=[shp] * 4, name="s5_prep",
    )(*ins)
    a_row = lambda t: t[::S5_GROUP].reshape(1, S5_WIDTH)
    eye = jnp.eye(S5_GROUP, dtype=F32)
    blk_b = lambda t: jnp.einsum('qgcn,gh->qgchn', t.reshape(S5_BLOCKS, 16, S5_GROUP, S5_STATE), eye
                                 ).reshape(S5_BLOCKS, S5_BLOCK_IN, S5_BLOCK_ST)
    wb = jnp.concatenate([blk_b(bbre), blk_b(bbim)], axis=2).astype(BF16)
    blk_c = lambda t: jnp.einsum('qgcn,gh->qhngc', t.reshape(S5_BLOCKS, 16, S5_GROUP, S5_STATE), eye
                                 ).reshape(S5_BLOCKS, S5_BLOCK_ST, S5_BLOCK_IN)
    wc = jnp.concatenate([blk_c(c_re), -blk_c(c_im)], axis=1).astype(BF16)
    return a_row(are), a_row(aim), wb, wc


def _s5_body(batch, x_ref, h0re_ref, h0im_ref, are_ref, aim_ref, wb_ref, wc_ref, d_ref, wv_ref, wg_ref,
             g_ref, b_ref, out_ref, hre_ref, him_ref, hbuf, sre, sim):
    rows = x_ref.shape[0]
    steps = rows // batch

    @pl.when(pl.program_id(0) == 0)
    def _():
        sre[...] = h0re_ref[...]
        sim[...] = h0im_ref[...]

    x = x_ref[...]
    xb = x.astype(BF16)
    for q in range(S5_BLOCKS):
        hbuf[:, q * 2 * S5_BLOCK_ST:(q + 1) * 2 * S5_BLOCK_ST] = _dot(
            xb[:, q * S5_BLOCK_IN:(q + 1) * S5_BLOCK_IN], wb_ref[q])

    def row_group(rg, carry):
        r_off = pl.multiple_of(rg * 8, 8)
        for q in range(S5_BLOCKS):
            st = slice(q * S5_BLOCK_ST, (q + 1) * S5_BLOCK_ST)
            c_re = slice(q * 2 * S5_BLOCK_ST, q * 2 * S5_BLOCK_ST + S5_BLOCK_ST)
            c_im = slice(q * 2 * S5_BLOCK_ST + S5_BLOCK_ST, (q + 1) * 2 * S5_BLOCK_ST)
            ar = jnp.broadcast_to(are_ref[:, st], (8, S5_BLOCK_ST))
            ai = jnp.broadcast_to(aim_ref[:, st], (8, S5_BLOCK_ST))

            def step(t, h):
                hr, hi = h
                row = pl.multiple_of(t * batch + r_off, 8)
                nr = ar * hr - ai * hi + hbuf[pl.ds(row, 8), c_re]
                ni = ar * hi + ai * hr + hbuf[pl.ds(row, 8), c_im]
                hbuf[pl.ds(row, 8), c_re] = nr
                hbuf[pl.ds(row, 8), c_im] = ni
                return nr, ni

            hr, hi = lax.fori_loop(0, steps, step, (sre[pl.ds(r_off, 8), st], sim[pl.ds(r_off, 8), st]))
            sre[pl.ds(r_off, 8), st] = hr
            sim[pl.ds(r_off, 8), st] = hi
        return carry

    lax.fori_loop(0, batch // 8, row_group, 0)
    hre_ref[...] = sre[...]
    him_ref[...] = sim[...]

    ys = [_dot(hbuf[:, q * 2 * S5_BLOCK_ST:(q + 1) * 2 * S5_BLOCK_ST].astype(BF16), wc_ref[q])
          for q in range(S5_BLOCKS)]
    y = jnp.concatenate(ys, axis=-1) + d_ref[...] * x
    z = jax.nn.gelu(y).astype(BF16)
    h = _dot(z, wv_ref[...]) * jax.nn.sigmoid(_dot(z, wg_ref[...]))
    out_ref[...] = _layer_norm(ALPHA * x + h, g_ref[...], b_ref[...])


def _s5_layer(x, batch, h0re, h0im, s5w, ln_g, ln_b):
    rows = x.shape[0]
    tm = min(ROW_TILE, rows)
    a_re, a_im, wb, wc, d_vec, wv, wg = s5w
    out, hre, him = _rows_call(
        functools.partial(_s5_body, batch), "s5_mixer", rows, tm, [x],
        [h0re, h0im, a_re, a_im, wb, wc, d_vec, wv, wg, ln_g, ln_b], [(D_MODEL, F32)],
        scratch=[pltpu.VMEM((tm, 2 * S5_WIDTH), F32), pltpu.VMEM((batch, S5_WIDTH), F32),
                 pltpu.VMEM((batch, S5_WIDTH), F32)],
        sequential=True, const_outs=[((batch, S5_WIDTH), F32)] * 2)
    return out, hre, him


def _head_sum(v, ones_ref):
    return _dot(v.astype(BF16), ones_ref[...])


def _rwkv_proj_body(batch, has_vgate, *refs):
    if has_vgate:
        (x_ref, vf_ref, shift_ref, mu_ref, wr_ref, wk_ref, wv_ref, w1_ref, w2_ref, a1_ref, a2_ref, g1_ref,
         g2_ref, v1_ref, v2_ref, vec_ref, ones_ref,
         r_out, lw_out, k_out, v_out, a_out, b_out, g_out, xprev) = refs
    else:
        (x_ref, shift_ref, mu_ref, wr_ref, wk_ref, wv_ref, w1_ref, w2_ref, a1_ref, a2_ref, g1_ref,
         g2_ref, vec_ref, ones_ref,
         r_out, lw_out, k_out, v_out, a_out, b_out, g_out, xprev) = refs
    rows = x_ref.shape[0]

    @pl.when(pl.program_id(0) == 0)
    def _():
        xprev[...] = shift_ref[...]

    x = x_ref[...]
    if rows > batch:
        xs = jnp.concatenate([xprev[...], x[:rows - batch]], axis=0)
    else:
        xs = xprev[...]
    xprev[...] = x[rows - batch:]
    xx = xs - x
    mix = lambda i: (x + xx * mu_ref[i:i + 1, :]).astype(BF16)
    w0, a0, v0, k_k, k_a = (vec_ref[i:i + 1, :] for i in range(5))

    r = _dot(mix(0), wr_ref[...])
    k = _dot(mix(2), wk_ref[...])
    xv = mix(3)
    v = _dot(xv, wv_ref[...])
    w_lora = _dot(jnp.tanh(_dot(mix(1), w1_ref[...])).astype(BF16), w2_ref[...])
    w_log = -jax.nn.softplus(-(w0 + w_lora)) - 0.5
    lw_out[...] = -jnp.exp(w_log)
    if has_vgate:
        gate_v = jax.nn.sigmoid(v0 + _dot(_dot(xv, v1_ref[...]).astype(BF16), v2_ref[...]))
        v = v + (vf_ref[...] - v) * gate_v
    a = jax.nn.sigmoid(a0 + _dot(_dot(mix(4), a1_ref[...]).astype(BF16), a2_ref[...]))
    g_out[...] = _dot(jax.nn.sigmoid(_dot(mix(5), g1_ref[...])).astype(BF16), g2_ref[...])
    kk = k * k_k
    sq = kk * kk
    sq_hi = sq.astype(BF16)
    ss = _dot(sq_hi, ones_ref[...]) + _head_sum(sq - sq_hi.astype(F32), ones_ref)
    kk = kk / jnp.maximum(jnp.sqrt(ss), 1e-12)
    r_out[...] = r
    k_out[...] = k * (1.0 + (a - 1.0) * k_a)
    v_out[...] = v
    a_out[...] = -kk
    b_out[...] = kk * a


def _rwkv_consts():
    L = RWKV_CHUNK
    lane = lax.broadcasted_iota(jnp.int32, (1, PAIR_W), 1)
    t_idx = lax.broadcasted_iota(jnp.int32, (L, PAIR_W), 0)
    s_idx = lax.broadcasted_iota(jnp.int32, (L, PAIR_W), 1) % L
    r2 = lax.broadcasted_iota(jnp.int32, (2 * L, PAIR_W), 0)
    c2 = lax.broadcasted_iota(jnp.int32, (2 * L, PAIR_W), 1)
    return dict(
        m0=lane < RWKV_HEAD,
        strict_lower=s_idx < t_idx, incl_lower=s_idx <= t_idx,
        same_head=(r2 < L) == (c2 < RWKV_HEAD),
        eye2=jnp.where(r2 == c2, 1.0, 0.0).astype(F32),
        tri=jnp.where(lax.broadcasted_iota(jnp.int32, (L, L), 1) <= lax.broadcasted_iota(jnp.int32, (L, L), 0),
                      1.0, 0.0).astype(BF16),
        zeros_l=jnp.zeros((L, PAIR_W), F32))


def _expand_state(sc, m0):
    return jnp.concatenate([jnp.where(m0, sc, 0.0), jnp.where(m0, 0.0, sc)], axis=0)


def _compact_state(s_bd):
    return s_bd[0:RWKV_HEAD] + s_bd[RWKV_HEAD:PAIR_W]


def _rwkv_chunk(slabs, states, cst):
    L = RWKV_CHUNK
    m0 = cst["m0"]
    bf = lambda t: t.astype(BF16)
    cat = lambda ts: jnp.concatenate(ts, axis=0)
    ap, rp, vp, btp, ktp, wbp, wkp, dtot = [], [], [], [], [], [], [], []
    for r, lw, k, v, a, b in slabs:
        h1 = bf(lw)
        e1 = lw - h1.astype(F32)
        h2 = bf(e1)
        h3 = bf(e1 - h2.astype(F32))
        cum = _dot(cst["tri"], h1) + _dot(cst["tri"], h2) + _dot(cst["tri"], h3)
        tot = cum[L - 1:L, :]
        inv = jnp.exp(-cum)
        tail = jnp.exp(tot - cum)
        pieces = (a * jnp.exp(cum - lw), r * jnp.exp(cum), v, b * inv, k * inv, b * tail, k * tail,
                  jnp.exp(tot))
        for p in range(r.shape[1] // PAIR_W):
            sl = slice(p * PAIR_W, (p + 1) * PAIR_W)
            for dst, src in zip((ap, rp, vp, btp, ktp, wbp, wkp, dtot), pieces):
                dst.append(src[:, sl])
    P = range(len(states))
    gram = [_dot_nt(bf(cat([jnp.where(m0, ap[p], 0.0), jnp.where(m0, 0.0, ap[p]),
                            jnp.where(m0, rp[p], 0.0), jnp.where(m0, 0.0, rp[p])])),
                    bf(cat([btp[p], ktp[p]]))) for p in P]
    ar_s = [_dot_nt(bf(cat([ap[p], rp[p]])), bf(states[p])) for p in P]
    na0 = [jnp.where(cst["strict_lower"], gram[p][0:L], 0.0) for p in P]
    na1 = [jnp.where(cst["strict_lower"], gram[p][L:2 * L], 0.0) for p in P]
    nr = [cat([jnp.where(cst["incl_lower"], gram[p][2 * L:3 * L], 0.0),
               jnp.where(cst["incl_lower"], gram[p][3 * L:4 * L], 0.0)]) for p in P]
    nv = [_dot(bf(cat([na0[p], na1[p]])), bf(cat([cst["zeros_l"], vp[p]]))) for p in P]
    pw = [cat([jnp.where(m0, na0[p], 0.0), jnp.where(m0, 0.0, pltpu.roll(na1[p], RWKV_HEAD, axis=1))])
          for p in P]
    t_inv = [cst["eye2"] + pw[p] for p in P]
    for _ in range(5):
        pwb = [bf(pw[p]) for p in P]
        pw = [_dot(pwb[p], pwb[p]) for p in P]
        t_inv = [t_inv[p] + _dot(bf(t_inv[p]), bf(pw[p])) for p in P]
    u_st = [_dot(bf(t_inv[p]), bf(cat([ar_s[p][0:L], ar_s[p][0:L]]) + nv[p])) for p in P]
    zz = [cat([jnp.where(m0, u_st[p][0:L], u_st[p][L:2 * L]), vp[p]]) for p in P]
    y_st = [_dot(bf(nr[p]), bf(zz[p])) for p in P]
    ys = [ar_s[p][L:2 * L] + jnp.where(m0, y_st[p][0:L], y_st[p][L:2 * L]) for p in P]
    upd = [_dot(bf(zz[p].T), bf(cat([wbp[p], wkp[p]]))) for p in P]
    new_states = [states[p] * dtot[p] + jnp.where(cst["same_head"], upd[p], 0.0) for p in P]
    y_slabs, at = [], 0
    for slab in slabs:
        n_i = slab[0].shape[1] // PAIR_W
        y_slabs.append(jnp.concatenate(ys[at:at + n_i], axis=1) if n_i > 1 else ys[at])
        at += n_i
    return y_slabs, new_states


def _rwkv_scan_seq_body(r_ref, lw_ref, k_ref, v_ref, a_ref, b_ref, s0_ref, y_ref, sout_ref):
    L = RWKV_CHUNK
    rows_in = r_ref.shape[0]
    cst = _rwkv_consts()
    pad = lambda ref: jnp.concatenate([ref[...], jnp.zeros((L - rows_in, D_MODEL), F32)], axis=0)
    slab = tuple(pad(ref) for ref in (r_ref, lw_ref, k_ref, v_ref, a_ref, b_ref))
    states = [_expand_state(s0_ref[p], cst["m0"]) for p in range(RWKV_PAIRS)]
    (y,), new_states = _rwkv_chunk([slab], states, cst)
    y_ref[...] = y[0:rows_in]
    for p in range(RWKV_PAIRS):
        sout_ref[p] = _compact_state(new_states[p])


def _rwkv_scan_tm_body(r_ref, lw_ref, k_ref, v_ref, a_ref, b_ref, s0_ref, y_ref, sout_ref, bufs, ybuf, s_scr):
    cst = _rwkv_consts()

    @pl.when(pl.program_id(1) == 0)
    def _():
        for b in range(SCAN_SEQS):
            for p in range(SCAN_PAIRS):
                s_scr[b * SCAN_PAIRS + p] = _expand_state(s0_ref[b, p], cst["m0"])

    for i, ref in enumerate((r_ref, lw_ref, k_ref, v_ref, a_ref, b_ref)):
        bufs[i] = _to_seq_major(ref[...])

    def two_sequences(i, carry):
        seqs = (2 * i, 2 * i + 1)
        slabs = [tuple(bufs[j, s] for j in range(6)) for s in seqs]
        states = [s_scr[s * SCAN_PAIRS + p] for s in seqs for p in range(SCAN_PAIRS)]
        y_slabs, new_states = _rwkv_chunk(slabs, states, cst)
        for n, s in enumerate(seqs):
            ybuf[s] = y_slabs[n]
            for p in range(SCAN_PAIRS):
                s_scr[s * SCAN_PAIRS + p] = new_states[n * SCAN_PAIRS + p]
        return carry

    lax.fori_loop(0, SCAN_SEQS // 2, two_sequences, 0)
    y_ref[...] = _to_time_major(ybuf[...])

    @pl.when(pl.program_id(1) == pl.num_programs(1) - 1)
    def _():
        for b in range(SCAN_SEQS):
            for p in range(SCAN_PAIRS):
                sout_ref[b, p] = _compact_state(s_scr[b * SCAN_PAIRS + p])


def _rwkv_out_body(y_ref, r_ref, k_ref, v_ref, g_ref, x_ref, ones_ref, vec_ref, wo_ref, lg_ref, lb_ref, out_ref):
    y = y_ref[...]
    r_k, lnx_g, lnx_b = (vec_ref[i:i + 1, :] for i in range(3))
    inv_n = 1.0 / RWKV_HEAD
    mu = _head_sum(y, ones_ref) * inv_n
    yc = y - mu
    var = _head_sum(yc * yc, ones_ref) * inv_n
    yn = yc * lax.rsqrt(var + RWKV_GN_EPS) * lnx_g + lnx_b
    bonus = _head_sum(r_ref[...] * k_ref[...] * r_k, ones_ref) * v_ref[...]
    o = ((yn + bonus) * g_ref[...]).astype(BF16)
    out_ref[...] = _layer_norm(ALPHA * x_ref[...] + _dot(o, wo_ref[...]), lg_ref[...], lb_ref[...])


def _pair_state(s):
    bsz = s.shape[0]
    s = s.reshape(bsz, RWKV_PAIRS, 2, RWKV_HEAD, RWKV_HEAD)
    return jnp.transpose(s, (0, 1, 3, 2, 4)).reshape(bsz, RWKV_PAIRS, RWKV_HEAD, PAIR_W)


def _unpair_state(s):
    bsz = s.shape[0]
    s = s.reshape(bsz, RWKV_PAIRS, RWKV_HEAD, 2, RWKV_HEAD)
    return jnp.transpose(s, (0, 1, 3, 2, 4)).reshape(bsz, RWKV_HEADS, RWKV_HEAD, RWKV_HEAD)


def _rwkv_layer(x, batch, t_len, s0, shift0, v_first, rw, ln_g, ln_b):
    rows = x.shape[0]
    tm = min(ROW_TILE, rows)
    has_vgate = v_first is not None
    row_ins = [x] + ([v_first] if has_vgate else [])
    consts = [shift0, rw["mu"], rw["w_r"], rw["w_k"], rw["w_v"], rw["w1"], rw["w2"], rw["a1"], rw["a2"],
              rw["g1"], rw["g2"]]
    if has_vgate:
        consts += [rw["v1"], rw["v2"]]
    consts += [rw["proj_vec"], rw["ones"]]
    r, lw, k, v, a, b, g = _rows_call(
        functools.partial(_rwkv_proj_body, batch, has_vgate), "rwkv_proj", rows, tm // 2, row_ins, consts,
        [(D_MODEL, F32)] * 7, scratch=[pltpu.VMEM((batch, D_MODEL), F32)], sequential=True)

    if batch == SCAN_SEQS:
        assert t_len % RWKV_CHUNK == 0
        n_half = D_MODEL // SCAN_LANES
        tile = pl.BlockSpec((SCAN_SEQS * RWKV_CHUNK, SCAN_LANES), lambda h, c: (c, h))
        st_spec = pl.BlockSpec((batch, None, SCAN_PAIRS, RWKV_HEAD, PAIR_W), lambda h, c: (0, h, 0, 0, 0))
        st_shape = (batch, n_half, SCAN_PAIRS, RWKV_HEAD, PAIR_W)
        y, s_last = pl.pallas_call(
            _rwkv_scan_tm_body, grid=(n_half, t_len // RWKV_CHUNK),
            in_specs=[tile] * 6 + [st_spec], out_specs=[tile, st_spec],
            out_shape=[jax.ShapeDtypeStruct((rows, D_MODEL), F32), jax.ShapeDtypeStruct(st_shape, F32)],
            scratch_shapes=[pltpu.VMEM((6, SCAN_SEQS, RWKV_CHUNK, SCAN_LANES), F32),
                            pltpu.VMEM((SCAN_SEQS, RWKV_CHUNK, SCAN_LANES), F32),
                            pltpu.VMEM((SCAN_SEQS * SCAN_PAIRS, PAIR_W, PAIR_W), F32)],
            compiler_params=_params("arbitrary", "arbitrary"), name="rwkv_scan",
        )(r, lw, k, v, a, b, _pair_state(s0).reshape(st_shape))
        s_last = s_last.reshape(batch, RWKV_PAIRS, RWKV_HEAD, PAIR_W)
    else:
        assert t_len <= RWKV_CHUNK
        view = lambda t: t.reshape(t_len, batch * D_MODEL)
        tile = pl.BlockSpec((t_len, D_MODEL), lambda bi: (0, bi))
        st_spec = pl.BlockSpec((None, RWKV_PAIRS, RWKV_HEAD, PAIR_W), lambda bi: (bi, 0, 0, 0))
        y2, s_last = pl.pallas_call(
            _rwkv_scan_seq_body, grid=(batch,),
            in_specs=[tile] * 6 + [st_spec], out_specs=[tile, st_spec],
            out_shape=[jax.ShapeDtypeStruct((t_len, batch * D_MODEL), F32),
                       jax.ShapeDtypeStruct((batch, RWKV_PAIRS, RWKV_HEAD, PAIR_W), F32)],
            compiler_params=_params("parallel"), name="rwkv_scan_seq",
        )(view(r), view(lw), view(k), view(v), view(a), view(b), _pair_state(s0))
        y = y2.reshape(rows, D_MODEL)

    out = _rows_call(_rwkv_out_body, "rwkv_out", rows, tm // 2, [y, r, k, v, g, x],
                     [rw["ones"], rw["out_vec"], rw["w_o"], ln_g, ln_b], [(D_MODEL, F32)])[0]
    return out, _unpair_state(s_last), x[rows - batch:], (v if not has_vgate else v_first)


def _trunk(x, batch, t_len, mem_k, mem_v, s5_re, s5_im, rwkv_s, shift, w):
    rows = x.shape[0]
    tm = min(ROW_TILE, rows)
    new_re, new_im, new_s, new_shift = [], [], [], []
    v_first = None
    for i in range(DEPTH):
        j = i // 2
        ln = lambda n: (w["ln_g"][i, n][None, :], w["ln_b"][i, n][None, :])
        if i % 2 == 0:
            x, hre, him = _s5_layer(x, batch, s5_re[j].reshape(batch, S5_WIDTH),
                                    s5_im[j].reshape(batch, S5_WIDTH), w["s5"][j], *ln(0))
            new_re.append(hre.reshape(batch, S5_GROUPS, S5_STATE))
            new_im.append(him.reshape(batch, S5_GROUPS, S5_STATE))
        else:
            x, s_last, x_last, v_first = _rwkv_layer(x, batch, t_len, rwkv_s[j], shift[j], v_first,
                                                     w["rwkv"][j], *ln(0))
            new_s.append(s_last)
            new_shift.append(x_last)
        if batch == SCAN_SEQS:
            x = _xattn_fused(x, w["xa_q"][i], mem_k, mem_v, w["xa_o"][i], *ln(1), i)
        else:
            q2 = _matmul(x, w["xa_q"][i], "xattn_q").reshape(t_len, batch * D_MODEL)
            o = _attention_cache(q2, mem_k, mem_v, i, batch, t_len).reshape(rows, D_MODEL)
            x = _rows_call(_proj_ln_body, "xattn_out", rows, tm, [x, o],
                           [w["xa_o"][i], *ln(1)], [(D_MODEL, F32)])[0]
        x = _rows_call(_mlp_body, "mlp", rows, tm, [x], [w["mlp1"][i], w["mlp2"][i], *ln(2)], [(D_MODEL, F32)])[0]
    return x, jnp.stack(new_re), jnp.stack(new_im), jnp.stack(new_s), jnp.stack(new_shift)


def _time_major(x):
    bsz, t_len, d = x.shape
    return jnp.transpose(x, (1, 0, 2)).reshape(t_len * bsz, d)


def _batch_major(x, bsz, t_len):
    return jnp.transpose(x.reshape(t_len, bsz, -1), (1, 0, 2))


def kernel(x_prompt, x_sample, mem_prompt, cache_mem_k, cache_mem_v, state_s5_re, state_s5_im, state_rwkv, state_shift, ln_g, ln_b, s5_a_re, s5_a_im, s5_log_dt, s5_b_re, s5_b_im, s5_c_re, s5_c_im, s5_d, s5_w_glu_v, s5_w_glu_g, rwkv_mu, rwkv_w_r, rwkv_w_k, rwkv_w_v, rwkv_w_o, rwkv_w0, rwkv_w1, rwkv_w2, rwkv_a0, rwkv_a1, rwkv_a2, rwkv_v0, rwkv_v1, rwkv_v2, rwkv_g1, rwkv_g2, rwkv_k_k, rwkv_k_a, rwkv_r_k, rwkv_lnx_g, rwkv_lnx_b, xa_w_q, xa_w_k, xa_w_v, xa_w_o, mlp_w1, mlp_w2):
    bf = lambda t: t.astype(BF16)
    n_s5, n_rwkv = state_s5_re.shape[0], state_rwkv.shape[0]
    b_p, t_p = x_prompt.shape[0], x_prompt.shape[1]
    b_s, t_s = x_sample.shape[0], x_sample.shape[1]

    head_id = jnp.arange(D_MODEL) // RWKV_HEAD
    ones_bd = (head_id[:, None] == head_id[None, :]).astype(BF16)
    w = {"ln_g": ln_g, "ln_b": ln_b, "s5": [], "rwkv": [],
         "xa_q": [bf(xa_w_q[i]) for i in range(DEPTH)], "xa_o": [bf(xa_w_o[i]) for i in range(DEPTH)],
         "mlp1": [bf(mlp_w1[i]) for i in range(DEPTH)], "mlp2": [bf(mlp_w2[i]) for i in range(DEPTH)]}
    for j in range(n_s5):
        a_re, a_im, wb, wc = _s5_prep(s5_a_re[j], s5_a_im[j], s5_log_dt[j], s5_b_re[j], s5_b_im[j],
                                      s5_c_re[j], s5_c_im[j])
        w["s5"].append((a_re, a_im, wb, wc, s5_d[j][None, :], bf(s5_w_glu_v[j]), bf(s5_w_glu_g[j])))
    for j in range(n_rwkv):
        v0 = rwkv_v0[j - 1] if j > 0 else jnp.zeros((D_MODEL,), F32)
        rw = {"mu": rwkv_mu[j], "w_r": bf(rwkv_w_r[j]), "w_k": bf(rwkv_w_k[j]), "w_v": bf(rwkv_w_v[j]),
              "w_o": bf(rwkv_w_o[j]), "w1": bf(rwkv_w1[j]), "w2": bf(rwkv_w2[j]), "a1": bf(rwkv_a1[j]),
              "a2": bf(rwkv_a2[j]), "g1": bf(rwkv_g1[j]), "g2": bf(rwkv_g2[j]), "ones": ones_bd,
              "proj_vec": jnp.stack([rwkv_w0[j], rwkv_a0[j], v0, rwkv_k_k[j], rwkv_k_a[j]]),
              "out_vec": jnp.stack([rwkv_r_k[j].reshape(D_MODEL), rwkv_lnx_g[j], rwkv_lnx_b[j]])}
        if j > 0:
            rw["v1"], rw["v2"] = bf(rwkv_v1[j - 1]), bf(rwkv_v2[j - 1])
        w["rwkv"].append(rw)

    mem_rows = mem_prompt.reshape(b_p * MEM_TOKENS, D_MODEL)
    mk = _mem_kv(mem_rows, xa_w_k, "mem_k").reshape(DEPTH, b_p, MEM_TOKENS, D_MODEL)
    mv = _mem_kv(mem_rows, xa_w_v, "mem_v").reshape(DEPTH, b_p, MEM_TOKENS, D_MODEL)
    z_s5 = jnp.zeros((n_s5, b_p, S5_GROUPS, S5_STATE), F32)
    z_rwkv = jnp.zeros((n_rwkv, b_p, RWKV_HEADS, RWKV_HEAD, RWKV_HEAD), F32)
    z_shift = jnp.zeros((n_rwkv, b_p, D_MODEL), F32)
    y_p, re_p, im_p, s_p, sh_p = _trunk(_time_major(x_prompt), b_p, t_p, bf(mk), bf(mv), z_s5, z_s5, z_rwkv,
                                        z_shift, w)

    y_s, re_s, im_s, s_s, sh_s = _trunk(_time_major(x_sample), b_s, t_s, cache_mem_k, cache_mem_v,
                                        state_s5_re, state_s5_im, state_rwkv, state_shift, w)

    kv_shape = (DEPTH, b_p, MEM_TOKENS, XA_HEADS, XA_HEAD_DIM)
    return (_batch_major(y_p, b_p, t_p), _batch_major(y_s, b_s, t_s), mk.reshape(kv_shape), mv.reshape(kv_shape),
            re_p, im_p, s_p, sh_p, re_s, im_s, s_s, sh_s)
```

```python
import functools

import jax
import jax.numpy as jnp
from jax import lax
from jax.experimental import pallas as pl
from jax.experimental.pallas import tpu as pltpu

F32 = jnp.float32
BF16 = jnp.bfloat16

D_MODEL = 1024
DEPTH = 4
S5_GROUP = 16
S5_GROUPS = 64
S5_STATE = 64
S5_WIDTH = S5_GROUPS * S5_STATE
S5_BLOCKS = 4
S5_BLOCK_IN = D_MODEL // S5_BLOCKS
S5_BLOCK_ST = S5_WIDTH // S5_BLOCKS
RWKV_HEADS = 16
RWKV_HEAD = 64
RWKV_PAIRS = RWKV_HEADS // 2
PAIR_W = 2 * RWKV_HEAD
RWKV_CHUNK = 64
RWKV_GN_EPS = 64e-5
SCAN_SEQS = 8
SCAN_LANES = 512
SCAN_PAIRS = SCAN_LANES // PAIR_W
MEM_TOKENS = 256
XA_HEADS = 4
XA_HEAD_DIM = 256
D_FF = 4 * D_MODEL
FF_CHUNK = 1024
LN_EPS = 1e-5
ALPHA = (2.0 * DEPTH) ** 0.25

V7X_VMEM_LIMIT = 56 * 1024 * 1024
ROW_TILE = 512
XA_ROWS = 1024


def _dot(a, b):
    return jnp.dot(a, b, preferred_element_type=F32)


def _dot_nt(a, b):
    return lax.dot_general(a, b, (((1,), (1,)), ((), ())), preferred_element_type=F32)


def _layer_norm(z, g, b):
    mu = jnp.mean(z, axis=-1, keepdims=True)
    zc = z - mu
    var = jnp.mean(zc * zc, axis=-1, keepdims=True)
    return zc * lax.rsqrt(var + LN_EPS) * g + b


def _to_seq_major(x):
    return jnp.transpose(x.reshape(x.shape[0] // 8, 8, x.shape[1]), (1, 0, 2))


def _to_time_major(x3):
    return jnp.transpose(x3, (1, 0, 2)).reshape(x3.shape[0] * x3.shape[1], x3.shape[2])


def _const_spec(shape):
    nd = len(shape)
    return pl.BlockSpec(tuple(shape), lambda *_: (0,) * nd, pipeline_mode=pl.Buffered(1))


def _params(*sem):
    return pltpu.CompilerParams(dimension_semantics=sem, vmem_limit_bytes=V7X_VMEM_LIMIT)


def _rows_call(body, name, rows, tm, row_ins, const_ins, outs, *, scratch=(), sequential=False,
               const_outs=(), seq_major_in=False, seq_major_out=False):
    seq_spec = lambda c: pl.BlockSpec((SCAN_SEQS, tm // SCAN_SEQS, c), lambda i: (0, i, 0))
    in_specs = [pl.BlockSpec((tm, a.shape[-1]), lambda i: (i, 0)) for a in row_ins]
    if seq_major_in:
        in_specs[0] = seq_spec(row_ins[0].shape[-1])
    in_specs += [_const_spec(a.shape) for a in const_ins]
    out_specs = [pl.BlockSpec((tm, c), lambda i: (i, 0)) for c, _ in outs]
    out_shape = [jax.ShapeDtypeStruct((rows, c), dt) for c, dt in outs]
    if seq_major_out:
        out_specs[0] = seq_spec(outs[0][0])
        out_shape[0] = jax.ShapeDtypeStruct((SCAN_SEQS, rows // SCAN_SEQS, outs[0][0]), outs[0][1])
    for shp, dt in const_outs:
        nd = len(shp)
        out_specs.append(pl.BlockSpec(tuple(shp), lambda i, nd=nd: (0,) * nd))
        out_shape.append(jax.ShapeDtypeStruct(tuple(shp), dt))
    return pl.pallas_call(
        body, grid=(rows // tm,), in_specs=in_specs, out_specs=out_specs, out_shape=out_shape,
        scratch_shapes=list(scratch),
        compiler_params=_params("arbitrary" if sequential else "parallel"), name=name,
    )(*row_ins, *const_ins)


def _matmul_body(x_ref, w_ref, o_ref):
    o_ref[...] = _dot(x_ref[...].astype(BF16), w_ref[...]).astype(o_ref.dtype)


def _matmul(x, w, name):
    rows = x.shape[0]
    return _rows_call(_matmul_body, name, rows, min(ROW_TILE, rows), [x], [w], [(w.shape[1], F32)])[0]


def _memkv_body(x_ref, w_ref, o_ref, o16_ref):
    y = _dot(x_ref[...].astype(BF16), w_ref[0].astype(BF16))
    o_ref[0] = y
    o16_ref[0] = y.astype(BF16)


def _mem_kv(mem_rows, w_stack, name):
    rows = mem_rows.shape[0]
    out_spec = pl.BlockSpec((1, rows, D_MODEL), lambda l: (l, 0, 0))
    return pl.pallas_call(
        _memkv_body, grid=(DEPTH,),
        in_specs=[_const_spec(mem_rows.shape), pl.BlockSpec((1, D_MODEL, D_MODEL), lambda l: (l, 0, 0))],
        out_specs=[out_spec, out_spec],
        out_shape=[jax.ShapeDtypeStruct((DEPTH, rows, D_MODEL), F32),
                   jax.ShapeDtypeStruct((DEPTH, rows, D_MODEL), BF16)],
        compiler_params=_params("parallel"), name=name,
    )(mem_rows, w_stack)


def _proj_ln_body(x_ref, o_ref, w_ref, g_ref, b_ref, out_ref):
    h = _dot(o_ref[...].astype(BF16), w_ref[...])
    out_ref[...] = _layer_norm(ALPHA * x_ref[...] + h, g_ref[...], b_ref[...])


def _mlp_body(x_ref, w1_ref, w2_ref, g_ref, b_ref, out_ref):
    x = x_ref[...]
    xb = x.astype(BF16)
    acc = None
    for c in range(D_FF // FF_CHUNK):
        h = _dot(xb, w1_ref[:, c * FF_CHUNK:(c + 1) * FF_CHUNK])
        h = jnp.maximum(h, 0.0)
        p = _dot((h * h).astype(BF16), w2_ref[c * FF_CHUNK:(c + 1) * FF_CHUNK, :])
        acc = p if acc is None else acc + p
    res = _layer_norm(ALPHA * x + acc, g_ref[...], b_ref[...])
    out_ref[...] = res if len(out_ref.shape) == 2 else _to_seq_major(res)


def _softmax_rows(s):
    m = jnp.max(s, axis=-1, keepdims=True)
    e = jnp.exp(s - m)
    return e / jnp.sum(e, axis=-1, keepdims=True)


def _xattn_fused_body(x_ref, wq_ref, k_ref, v_ref, wo_ref, g_ref, b_ref, out_ref):
    x = x_ref[...]
    q3 = _to_seq_major(_dot(x.astype(BF16), wq_ref[...])).astype(BF16)
    heads = [slice(h * XA_HEAD_DIM, (h + 1) * XA_HEAD_DIM) for h in range(XA_HEADS)]
    per_seq = []
    for b in range(SCAN_SEQS):
        qb, kb, vb = q3[b], k_ref[b], v_ref[b]
        s = [_dot_nt(qb[:, sl], kb[:, sl]) * (XA_HEAD_DIM ** -0.5) for sl in heads]
        p = [_softmax_rows(t).astype(BF16) for t in s]
        per_seq.append(jnp.concatenate([_dot(p[h], vb[:, heads[h]]) for h in range(XA_HEADS)], axis=-1))
    o = _to_time_major(jnp.stack(per_seq, axis=0))
    out_ref[...] = _layer_norm(ALPHA * x + _dot(o.astype(BF16), wo_ref[...]), g_ref[...], b_ref[...])


def _xattn_fused(x, wq, mem_k, mem_v, wo, ln_g, ln_b, layer):
    rows = x.shape[0]
    tm = min(XA_ROWS, rows)
    tile = pl.BlockSpec((tm, D_MODEL), lambda i: (i, 0))
    kv_spec = pl.BlockSpec((None, SCAN_SEQS, MEM_TOKENS, D_MODEL), lambda i: (layer, 0, 0, 0),
                           pipeline_mode=pl.Buffered(1))
    vec = _const_spec((1, D_MODEL))
    return pl.pallas_call(
        _xattn_fused_body, grid=(rows // tm,),
        in_specs=[tile, _const_spec(wq.shape), kv_spec, kv_spec, _const_spec(wo.shape), vec, vec],
        out_specs=tile, out_shape=jax.ShapeDtypeStruct((rows, D_MODEL), F32),
        compiler_params=_params("parallel"), name="xattn_fused",
    )(x, wq, mem_k, mem_v, wo, ln_g, ln_b)


def _attn_cache_body(q_ref, k_ref, v_ref, o_ref):
    t_len, n_seq = q_ref.shape[0], q_ref.shape[1]
    shape = (XA_HEADS * t_len, MEM_TOKENS * XA_HEADS)
    own_head = (lax.broadcasted_iota(jnp.int32, shape, 0) // t_len
                == lax.broadcasted_iota(jnp.int32, shape, 1) % XA_HEADS)
    q3 = _to_seq_major(q_ref[...].reshape(t_len * n_seq, D_MODEL))
    outs = []
    for b in range(n_seq):
        q = q3[b]
        qs = jnp.concatenate([q[:, h * XA_HEAD_DIM:(h + 1) * XA_HEAD_DIM] for h in range(XA_HEADS)], axis=0)
        k2 = k_ref[b].reshape(MEM_TOKENS * XA_HEADS, XA_HEAD_DIM).astype(BF16)
        v2 = v_ref[b].reshape(MEM_TOKENS * XA_HEADS, XA_HEAD_DIM).astype(BF16)
        s = _dot_nt(qs.astype(BF16), k2) * (XA_HEAD_DIM ** -0.5)
        p = _softmax_rows(jnp.where(own_head, s, -1e30))
        o = _dot(p.astype(BF16), v2)
        outs.append(jnp.concatenate([o[h * t_len:(h + 1) * t_len] for h in range(XA_HEADS)], axis=-1))
    o_ref[...] = _to_time_major(jnp.stack(outs, axis=0)).reshape(t_len, n_seq, D_MODEL)


def _attention_cache(q_rows, cache_k, cache_v, layer, batch, t_len):
    nb = SCAN_SEQS
    assert t_len == 8 and batch % nb == 0
    kv_spec = pl.BlockSpec((None, nb, MEM_TOKENS, XA_HEADS, XA_HEAD_DIM), lambda b: (layer, b, 0, 0, 0))
    tile = pl.BlockSpec((t_len, nb, D_MODEL), lambda b: (0, b, 0))
    return pl.pallas_call(
        _attn_cache_body, grid=(batch // nb,), in_specs=[tile, kv_spec, kv_spec], out_specs=tile,
        out_shape=jax.ShapeDtypeStruct((t_len, batch, D_MODEL), F32),
        compiler_params=_params("parallel"), name="xattn_cache",
    )(q_rows.reshape(t_len, batch, D_MODEL), cache_k, cache_v).reshape(t_len * batch, D_MODEL)


def _s5_prep_body(lre_ref, lim_ref, ldt_ref, br_ref, bi_ref, are_ref, aim_ref, bbre_ref, bbim_ref):
    lre = lre_ref[...]
    lim = lim_ref[...]
    dt = jnp.exp(ldt_ref[...])
    mag = jnp.exp(lre * dt)
    ang = lim * dt
    ab_re = mag * jnp.cos(ang)
    ab_im = mag * jnp.sin(ang)
    den = lre * lre + lim * lim
    num_re = ab_re - 1.0
    f_re = (num_re * lre + ab_im * lim) / den
    f_im = (ab_im * lre - num_re * lim) / den
    br = br_ref[...]
    bi = bi_ref[...]
    are_ref[...] = ab_re
    aim_ref[...] = ab_im
    bbre_ref[...] = f_re * br - f_im * bi
    bbim_ref[...] = f_re * bi + f_im * br


def _s5_prep(a_re, a_im, log_dt, b_re, b_im, c_re, c_im):
    rep = lambda t: jnp.repeat(t, S5_GROUP, axis=0)
    to_rows = lambda t: jnp.transpose(t, (0, 2, 1)).reshape(S5_GROUPS * S5_GROUP, S5_STATE)
    ins = [rep(a_re), rep(a_im), rep(jnp.broadcast_to(log_dt[:, None], (S5_GROUPS, S5_STATE))),
           to_rows(b_re), to_rows(b_im)]
    shp = jax.ShapeDtypeStruct((S5_GROUPS * S5_GROUP, S5_STATE), F32)
    full = pl.BlockSpec((S5_GROUPS * S5_GROUP, S5_STATE), lambda: (0, 0))
    are, aim, bbre, bbim = pl.pallas_call(
        _s5_prep_body, in_specs=[full] * 5, out_specs=[full] * 4, out_shape=[shp] * 4, name="s5_prep",
    )(*ins)
    a_row = lambda t: t[::S5_GROUP].reshape(1, S5_WIDTH)
    eye = jnp.eye(S5_GROUP, dtype=F32)
    blk_b = lambda t: jnp.einsum('qgcn,gh->qgchn', t.reshape(S5_BLOCKS, 16, S5_GROUP, S5_STATE), eye
                                 ).reshape(S5_BLOCKS, S5_BLOCK_IN, S5_BLOCK_ST)
    wb = jnp.concatenate([blk_b(bbre), blk_b(bbim)], axis=2).astype(BF16)
    blk_c = lambda t: jnp.einsum('qgcn,gh->qhngc', t.reshape(S5_BLOCKS, 16, S5_GROUP, S5_STATE), eye
                                 ).reshape(S5_BLOCKS, S5_BLOCK_ST, S5_BLOCK_IN)
    wc = jnp.concatenate([blk_c(c_re), -blk_c(c_im)], axis=1).astype(BF16)
    return a_row(are), a_row(aim), wb, wc


def _s5_body(batch, x_ref, h0re_ref, h0im_ref, are_ref, aim_ref, wb_ref, wc_ref, d_ref, wv_ref, wg_ref,
             g_ref, b_ref, out_ref, hre_ref, him_ref, hbuf, sre, sim):
    x = x_ref[...] if len(x_ref.shape) == 2 else _to_time_major(x_ref[...])
    rows = x.shape[0]
    steps = rows // batch

    @pl.when(pl.program_id(0) == 0)
    def _():
        sre[...] = h0re_ref[...]
        sim[...] = h0im_ref[...]

    xb = x.astype(BF16)
    for q in range(S5_BLOCKS):
        hbuf[:, q * 2 * S5_BLOCK_ST:(q + 1) * 2 * S5_BLOCK_ST] = _dot(
            xb[:, q * S5_BLOCK_IN:(q + 1) * S5_BLOCK_IN], wb_ref[q])

    def row_group(rg, carry):
        r_off = pl.multiple_of(rg * 8, 8)
        for q in range(S5_BLOCKS):
            st = slice(q * S5_BLOCK_ST, (q + 1) * S5_BLOCK_ST)
            c_re = slice(q * 2 * S5_BLOCK_ST, q * 2 * S5_BLOCK_ST + S5_BLOCK_ST)
            c_im = slice(q * 2 * S5_BLOCK_ST + S5_BLOCK_ST, (q + 1) * 2 * S5_BLOCK_ST)
            ar = jnp.broadcast_to(are_ref[:, st], (8, S5_BLOCK_ST))
            ai = jnp.broadcast_to(aim_ref[:, st], (8, S5_BLOCK_ST))

            def step(t, h):
                hr, hi = h
                row = pl.multiple_of(t * batch + r_off, 8)
                nr = ar * hr - ai * hi + hbuf[pl.ds(row, 8), c_re]
                ni = ar * hi + ai * hr + hbuf[pl.ds(row, 8), c_im]
                hbuf[pl.ds(row, 8), c_re] = nr
                hbuf[pl.ds(row, 8), c_im] = ni
                return nr, ni

            hr, hi = lax.fori_loop(0, steps, step, (sre[pl.ds(r_off, 8), st], sim[pl.ds(r_off, 8), st]))
            sre[pl.ds(r_off, 8), st] = hr
            sim[pl.ds(r_off, 8), st] = hi
        return carry

    lax.fori_loop(0, batch // 8, row_group, 0)
    hre_ref[...] = sre[...]
    him_ref[...] = sim[...]

    ys = [_dot(hbuf[:, q * 2 * S5_BLOCK_ST:(q + 1) * 2 * S5_BLOCK_ST].astype(BF16), wc_ref[q])
          for q in range(S5_BLOCKS)]
    y = jnp.concatenate(ys, axis=-1) + d_ref[...] * x
    z = jax.nn.gelu(y).astype(BF16)
    h = _dot(z, wv_ref[...]) * jax.nn.sigmoid(_dot(z, wg_ref[...]))
    out_ref[...] = _layer_norm(ALPHA * x + h, g_ref[...], b_ref[...])


def _s5_layer(x, batch, h0re, h0im, s5w, ln_g, ln_b):
    rows = x.size // D_MODEL
    tm = min(ROW_TILE, rows)
    a_re, a_im, wb, wc, d_vec, wv, wg = s5w
    out, hre, him = _rows_call(
        functools.partial(_s5_body, batch), "s5_mixer", rows, tm, [x],
        [h0re, h0im, a_re, a_im, wb, wc, d_vec, wv, wg, ln_g, ln_b], [(D_MODEL, F32)],
        scratch=[pltpu.VMEM((tm, 2 * S5_WIDTH), F32), pltpu.VMEM((batch, S5_WIDTH), F32),
                 pltpu.VMEM((batch, S5_WIDTH), F32)],
        sequential=True, const_outs=[((batch, S5_WIDTH), F32)] * 2, seq_major_in=(x.ndim == 3))
    return out, hre, him


def _head_sum(v, ones_ref):
    return _dot(v.astype(BF16), ones_ref[...])


def _rwkv_proj_body(batch, has_vgate, *refs):
    if has_vgate:
        (x_ref, vf_ref, shift_ref, mu_ref, wr_ref, wk_ref, wv_ref, w1_ref, w2_ref, a1_ref, a2_ref, g1_ref,
         g2_ref, v1_ref, v2_ref, vec_ref, ones_ref,
         r_out, lw_out, k_out, v_out, a_out, b_out, g_out, xprev) = refs
    else:
        (x_ref, shift_ref, mu_ref, wr_ref, wk_ref, wv_ref, w1_ref, w2_ref, a1_ref, a2_ref, g1_ref,
         g2_ref, vec_ref, ones_ref,
         r_out, lw_out, k_out, v_out, a_out, b_out, g_out, xprev) = refs
    rows = x_ref.shape[0]

    @pl.when(pl.program_id(0) == 0)
    def _():
        xprev[...] = shift_ref[...]

    x = x_ref[...]
    if rows > batch:
        xs = jnp.concatenate([xprev[...], x[:rows - batch]], axis=0)
    else:
        xs = xprev[...]
    xprev[...] = x[rows - batch:]
    xx = xs - x
    mix = lambda i: (x + xx * mu_ref[i:i + 1, :]).astype(BF16)
    w0, a0, v0, k_k, k_a = (vec_ref[i:i + 1, :] for i in range(5))

    r = _dot(mix(0), wr_ref[...])
    k = _dot(mix(2), wk_ref[...])
    xv = mix(3)
    v = _dot(xv, wv_ref[...])
    w_lora = _dot(jnp.tanh(_dot(mix(1), w1_ref[...])).astype(BF16), w2_ref[...])
    w_log = -jax.nn.softplus(-(w0 + w_lora)) - 0.5
    lw_out[...] = -jnp.exp(w_log)
    if has_vgate:
        gate_v = jax.nn.sigmoid(v0 + _dot(_dot(xv, v1_ref[...]).astype(BF16), v2_ref[...]))
        v = v + (vf_ref[...] - v) * gate_v
    a = jax.nn.sigmoid(a0 + _dot(_dot(mix(4), a1_ref[...]).astype(BF16), a2_ref[...]))
    g_out[...] = _dot(jax.nn.sigmoid(_dot(mix(5), g1_ref[...])).astype(BF16), g2_ref[...])
    kk = k * k_k
    sq = kk * kk
    sq_hi = sq.astype(BF16)
    ss = _dot(sq_hi, ones_ref[...]) + _head_sum(sq - sq_hi.astype(F32), ones_ref)
    kk = kk / jnp.maximum(jnp.sqrt(ss), 1e-12)
    r_out[...] = r
    k_out[...] = k * (1.0 + (a - 1.0) * k_a)
    v_out[...] = v
    a_out[...] = -kk
    b_out[...] = kk * a


def _rwkv_consts(seq_len=RWKV_CHUNK):
    L = RWKV_CHUNK
    lane = lax.broadcasted_iota(jnp.int32, (1, PAIR_W), 1)
    t_idx = lax.broadcasted_iota(jnp.int32, (L, PAIR_W), 0)
    s_idx = lax.broadcasted_iota(jnp.int32, (L, PAIR_W), 1) % L
    r2 = lax.broadcasted_iota(jnp.int32, (2 * L, PAIR_W), 0)
    c2 = lax.broadcasted_iota(jnp.int32, (2 * L, PAIR_W), 1)
    rr = lax.broadcasted_iota(jnp.int32, (L, L), 0)
    cc = lax.broadcasted_iota(jnp.int32, (L, L), 1)
    strict_lower, incl_lower, tri = s_idx < t_idx, s_idx <= t_idx, cc <= rr
    cst = {}
    if seq_len < L:
        same_seq = (s_idx // seq_len) == (t_idx // seq_len)
        strict_lower = same_seq & strict_lower
        incl_lower = same_seq & incl_lower
        tri = ((cc // seq_len) == (rr // seq_len)) & tri
        cst["seq_ones"] = jnp.where((cc // seq_len) == (rr // seq_len), 1.0, 0.0).astype(BF16)
        cst["col_seq"] = (c2 % L) // seq_len
    cst.update(
        m0=lane < RWKV_HEAD,
        strict_lower=strict_lower, incl_lower=incl_lower,
        same_head=(r2 < L) == (c2 < RWKV_HEAD),
        eye2=jnp.where(r2 == c2, 1.0, 0.0).astype(F32),
        tri=jnp.where(tri, 1.0, 0.0).astype(BF16),
        zeros_l=jnp.zeros((L, PAIR_W), F32))
    return cst


def _expand_state(sc, m0):
    return jnp.concatenate([jnp.where(m0, sc, 0.0), jnp.where(m0, 0.0, sc)], axis=0)


def _compact_state(s_bd):
    return s_bd[0:RWKV_HEAD] + s_bd[RWKV_HEAD:PAIR_W]


def _rwkv_chunk(slabs, states, cst, seq_len=RWKV_CHUNK):
    L = RWKV_CHUNK
    n_seq = L // seq_len
    m0 = cst["m0"]
    bf = lambda t: t.astype(BF16)
    cat = lambda ts: jnp.concatenate(ts, axis=0)
    ap, rp, vp, btp, ktp, wbp, wkp, dtot = [], [], [], [], [], [], [], []
    for r, lw, k, v, a, b in slabs:
        h1 = bf(lw)
        e1 = lw - h1.astype(F32)
        h2 = bf(e1)
        h3 = bf(e1 - h2.astype(F32))
        cum = _dot(cst["tri"], h1) + _dot(cst["tri"], h2) + _dot(cst["tri"], h3)
        if n_seq == 1:
            tot = cum[L - 1:L, :]
        else:
            tot = _dot(cst["seq_ones"], h1) + _dot(cst["seq_ones"], h2) + _dot(cst["seq_ones"], h3)
        inv = jnp.exp(-cum)
        tail = jnp.exp(tot - cum)
        pieces = (a * jnp.exp(cum - lw), r * jnp.exp(cum), v, b * inv, k * inv, b * tail, k * tail,
                  jnp.exp(tot))
        for p in range(r.shape[1] // PAIR_W):
            sl = slice(p * PAIR_W, (p + 1) * PAIR_W)
            for dst, src in zip((ap, rp, vp, btp, ktp, wbp, wkp, dtot), pieces):
                dst.append(src[:, sl])
    P = range(len(ap))
    gram = [_dot_nt(bf(cat([jnp.where(m0, ap[p], 0.0), jnp.where(m0, 0.0, ap[p]),
                            jnp.where(m0, rp[p], 0.0), jnp.where(m0, 0.0, rp[p])])),
                    bf(cat([btp[p], ktp[p]]))) for p in P]
    if n_seq == 1:
        ar_s = [_dot_nt(bf(cat([ap[p], rp[p]])), bf(states[p])) for p in P]
    else:
        ar_s = []
        for p in P:
            parts = [_dot_nt(bf(cat([ap[p][s * seq_len:(s + 1) * seq_len], rp[p][s * seq_len:(s + 1) * seq_len]])),
                             bf(states[p * n_seq + s])) for s in range(n_seq)]
            ar_s.append(cat([o[0:seq_len] for o in parts] + [o[seq_len:2 * seq_len] for o in parts]))
    na0 = [jnp.where(cst["strict_lower"], gram[p][0:L], 0.0) for p in P]
    na1 = [jnp.where(cst["strict_lower"], gram[p][L:2 * L], 0.0) for p in P]
    nr = [cat([jnp.where(cst["incl_lower"], gram[p][2 * L:3 * L], 0.0),
               jnp.where(cst["incl_lower"], gram[p][3 * L:4 * L], 0.0)]) for p in P]
    nv = [_dot(bf(cat([na0[p], na1[p]])), bf(cat([cst["zeros_l"], vp[p]]))) for p in P]
    pw = [cat([jnp.where(m0, na0[p], 0.0), jnp.where(m0, 0.0, pltpu.roll(na1[p], RWKV_HEAD, axis=1))])
          for p in P]
    t_inv = [cst["eye2"] + pw[p] for p in P]
    for _ in range(seq_len.bit_length() - 2):
        pwb = [bf(pw[p]) for p in P]
        pw = [_dot(pwb[p], pwb[p]) for p in P]
        t_inv = [t_inv[p] + _dot(bf(t_inv[p]), bf(pw[p])) for p in P]
    u_st = [_dot(bf(t_inv[p]), bf(cat([ar_s[p][0:L], ar_s[p][0:L]]) + nv[p])) for p in P]
    zz = [cat([jnp.where(m0, u_st[p][0:L], u_st[p][L:2 * L]), vp[p]]) for p in P]
    y_st = [_dot(bf(nr[p]), bf(zz[p])) for p in P]
    ys = [ar_s[p][L:2 * L] + jnp.where(m0, y_st[p][0:L], y_st[p][L:2 * L]) for p in P]
    if n_seq == 1:
        upd = [_dot(bf(zz[p].T), bf(cat([wbp[p], wkp[p]]))) for p in P]
        new_states = [states[p] * dtot[p] + jnp.where(cst["same_head"], upd[p], 0.0) for p in P]
    else:
        new_states = []
        for p in P:
            zz_t = zz[p].T
            lhs = cat([jnp.where(cst["col_seq"] == s, zz_t, 0.0) for s in range(n_seq)])
            upd = _dot(bf(lhs), bf(cat([wbp[p], wkp[p]])))
            for s in range(n_seq):
                new_states.append(states[p * n_seq + s] * dtot[p][s * seq_len:s * seq_len + 1]
                                  + jnp.where(cst["same_head"], upd[s * PAIR_W:(s + 1) * PAIR_W], 0.0))
    y_slabs, at = [], 0
    for slab in slabs:
        n_i = slab[0].shape[1] // PAIR_W
        y_slabs.append(jnp.concatenate(ys[at:at + n_i], axis=1) if n_i > 1 else ys[at])
        at += n_i
    return y_slabs, new_states


def _rwkv_scan_short_body(r_ref, lw_ref, k_ref, v_ref, a_ref, b_ref, s0_ref, y_ref, sout_ref):
    t_len, n_seq = r_ref.shape[0], r_ref.shape[1]
    cst = _rwkv_consts(t_len)
    rows = lambda ref: _to_seq_major(ref[...].reshape(t_len * n_seq, D_MODEL)).reshape(n_seq * t_len, D_MODEL)
    slab = tuple(rows(ref) for ref in (r_ref, lw_ref, k_ref, v_ref, a_ref, b_ref))
    states = [_expand_state(s0_ref[s, p], cst["m0"]) for p in range(RWKV_PAIRS) for s in range(n_seq)]
    (y,), new_states = _rwkv_chunk([slab], states, cst, t_len)
    y_ref[...] = _to_time_major(y.reshape(n_seq, t_len, D_MODEL)).reshape(t_len, n_seq, D_MODEL)
    for s in range(n_seq):
        for p in range(RWKV_PAIRS):
            sout_ref[s, p] = _compact_state(new_states[p * n_seq + s])


def _rwkv_scan_tm_body(r_ref, lw_ref, k_ref, v_ref, a_ref, b_ref, s0_ref, y_ref, sout_ref, bufs, ybuf, s_scr):
    cst = _rwkv_consts()

    @pl.when(pl.program_id(1) == 0)
    def _():
        for b in range(SCAN_SEQS):
            for p in range(SCAN_PAIRS):
                s_scr[b * SCAN_PAIRS + p] = _expand_state(s0_ref[b, p], cst["m0"])

    for i, ref in enumerate((r_ref, lw_ref, k_ref, v_ref, a_ref, b_ref)):
        bufs[i] = _to_seq_major(ref[...])

    def two_sequences(i, carry):
        seqs = (2 * i, 2 * i + 1)
        slabs = [tuple(bufs[j, s] for j in range(6)) for s in seqs]
        states = [s_scr[s * SCAN_PAIRS + p] for s in seqs for p in range(SCAN_PAIRS)]
        y_slabs, new_states = _rwkv_chunk(slabs, states, cst)
        for n, s in enumerate(seqs):
            ybuf[s] = y_slabs[n]
            for p in range(SCAN_PAIRS):
                s_scr[s * SCAN_PAIRS + p] = new_states[n * SCAN_PAIRS + p]
        return carry

    lax.fori_loop(0, SCAN_SEQS // 2, two_sequences, 0)
    y_ref[...] = _to_time_major(ybuf[...])

    @pl.when(pl.program_id(1) == pl.num_programs(1) - 1)
    def _():
        for b in range(SCAN_SEQS):
            for p in range(SCAN_PAIRS):
                sout_ref[b, p] = _compact_state(s_scr[b * SCAN_PAIRS + p])


def _rwkv_out_body(y_ref, r_ref, k_ref, v_ref, g_ref, x_ref, ones_ref, vec_ref, wo_ref, lg_ref, lb_ref, out_ref):
    y = y_ref[...]
    r_k, lnx_g, lnx_b = (vec_ref[i:i + 1, :] for i in range(3))
    inv_n = 1.0 / RWKV_HEAD
    mu = _head_sum(y, ones_ref) * inv_n
    yc = y - mu
    var = _head_sum(yc * yc, ones_ref) * inv_n
    yn = yc * lax.rsqrt(var + RWKV_GN_EPS) * lnx_g + lnx_b
    bonus = _head_sum(r_ref[...] * k_ref[...] * r_k, ones_ref) * v_ref[...]
    o = ((yn + bonus) * g_ref[...]).astype(BF16)
    out_ref[...] = _layer_norm(ALPHA * x_ref[...] + _dot(o, wo_ref[...]), lg_ref[...], lb_ref[...])


def _pair_state(s):
    bsz = s.shape[0]
    s = s.reshape(bsz, RWKV_PAIRS, 2, RWKV_HEAD, RWKV_HEAD)
    return jnp.transpose(s, (0, 1, 3, 2, 4)).reshape(bsz, RWKV_PAIRS, RWKV_HEAD, PAIR_W)


def _unpair_state(s):
    bsz = s.shape[0]
    s = s.reshape(bsz, RWKV_PAIRS, RWKV_HEAD, 2, RWKV_HEAD)
    return jnp.transpose(s, (0, 1, 3, 2, 4)).reshape(bsz, RWKV_HEADS, RWKV_HEAD, RWKV_HEAD)


def _rwkv_layer(x, batch, t_len, s0, shift0, v_first, rw, ln_g, ln_b):
    rows = x.shape[0]
    tm = min(ROW_TILE, rows)
    has_vgate = v_first is not None
    row_ins = [x] + ([v_first] if has_vgate else [])
    consts = [shift0, rw["mu"], rw["w_r"], rw["w_k"], rw["w_v"], rw["w1"], rw["w2"], rw["a1"], rw["a2"],
              rw["g1"], rw["g2"]]
    if has_vgate:
        consts += [rw["v1"], rw["v2"]]
    consts += [rw["proj_vec"], rw["ones"]]
    r, lw, k, v, a, b, g = _rows_call(
        functools.partial(_rwkv_proj_body, batch, has_vgate), "rwkv_proj", rows, tm // 2, row_ins, consts,
        [(D_MODEL, F32)] * 7, scratch=[pltpu.VMEM((batch, D_MODEL), F32)], sequential=True)

    if batch == SCAN_SEQS:
        assert t_len % RWKV_CHUNK == 0
        n_half = D_MODEL // SCAN_LANES
        tile = pl.BlockSpec((SCAN_SEQS * RWKV_CHUNK, SCAN_LANES), lambda h, c: (c, h))
        st_spec = pl.BlockSpec((batch, None, SCAN_PAIRS, RWKV_HEAD, PAIR_W), lambda h, c: (0, h, 0, 0, 0))
        st_shape = (batch, n_half, SCAN_PAIRS, RWKV_HEAD, PAIR_W)
        y, s_last = pl.pallas_call(
            _rwkv_scan_tm_body, grid=(n_half, t_len // RWKV_CHUNK),
            in_specs=[tile] * 6 + [st_spec], out_specs=[tile, st_spec],
            out_shape=[jax.ShapeDtypeStruct((rows, D_MODEL), F32), jax.ShapeDtypeStruct(st_shape, F32)],
            scratch_shapes=[pltpu.VMEM((6, SCAN_SEQS, RWKV_CHUNK, SCAN_LANES), F32),
                            pltpu.VMEM((SCAN_SEQS, RWKV_CHUNK, SCAN_LANES), F32),
                            pltpu.VMEM((SCAN_SEQS * SCAN_PAIRS, PAIR_W, PAIR_W), F32)],
            compiler_params=_params("arbitrary", "arbitrary"), name="rwkv_scan",
        )(r, lw, k, v, a, b, _pair_state(s0).reshape(st_shape))
        s_last = s_last.reshape(batch, RWKV_PAIRS, RWKV_HEAD, PAIR_W)
    else:
        n_seq = SCAN_SEQS
        assert t_len * n_seq == RWKV_CHUNK and batch % n_seq == 0
        view = lambda t: t.reshape(t_len, batch, D_MODEL)
        tile = pl.BlockSpec((t_len, n_seq, D_MODEL), lambda bi: (0, bi, 0))
        st_spec = pl.BlockSpec((n_seq, RWKV_PAIRS, RWKV_HEAD, PAIR_W), lambda bi: (bi, 0, 0, 0))
        y2, s_last = pl.pallas_call(
            _rwkv_scan_short_body, grid=(batch // n_seq,),
            in_specs=[tile] * 6 + [st_spec], out_specs=[tile, st_spec],
            out_shape=[jax.ShapeDtypeStruct((t_len, batch, D_MODEL), F32),
                       jax.ShapeDtypeStruct((batch, RWKV_PAIRS, RWKV_HEAD, PAIR_W), F32)],
            compiler_params=_params("parallel"), name="rwkv_scan_short",
        )(view(r), view(lw), view(k), view(v), view(a), view(b), _pair_state(s0))
        y = y2.reshape(rows, D_MODEL)

    out = _rows_call(_rwkv_out_body, "rwkv_out", rows, tm // 2, [y, r, k, v, g, x],
                     [rw["ones"], rw["out_vec"], rw["w_o"], ln_g, ln_b], [(D_MODEL, F32)])[0]
    return out, _unpair_state(s_last), x[rows - batch:], (v if not has_vgate else v_first)


def _trunk(x, batch, t_len, mem_k, mem_v, s5_re, s5_im, rwkv_s, shift, w):
    rows = x.size // D_MODEL
    tm = min(ROW_TILE, rows)
    new_re, new_im, new_s, new_shift = [], [], [], []
    v_first = None
    for i in range(DEPTH):
        j = i // 2
        ln = lambda n: (w["ln_g"][i, n][None, :], w["ln_b"][i, n][None, :])
        if i % 2 == 0:
            x, hre, him = _s5_layer(x, batch, s5_re[j].reshape(batch, S5_WIDTH),
                                    s5_im[j].reshape(batch, S5_WIDTH), w["s5"][j], *ln(0))
            new_re.append(hre.reshape(batch, S5_GROUPS, S5_STATE))
            new_im.append(him.reshape(batch, S5_GROUPS, S5_STATE))
        else:
            x, s_last, x_last, v_first = _rwkv_layer(x, batch, t_len, rwkv_s[j], shift[j], v_first,
                                                     w["rwkv"][j], *ln(0))
            new_s.append(s_last)
            new_shift.append(x_last)
        if batch == SCAN_SEQS:
            x = _xattn_fused(x, w["xa_q"][i], mem_k, mem_v, w["xa_o"][i], *ln(1), i)
        else:
            o = _attention_cache(_matmul(x, w["xa_q"][i], "xattn_q"), mem_k, mem_v, i, batch, t_len)
            x = _rows_call(_proj_ln_body, "xattn_out", rows, tm, [x, o],
                           [w["xa_o"][i], *ln(1)], [(D_MODEL, F32)])[0]
        x = _rows_call(_mlp_body, "mlp", rows, tm, [x], [w["mlp1"][i], w["mlp2"][i], *ln(2)], [(D_MODEL, F32)],
                       seq_major_out=(batch == SCAN_SEQS and i == DEPTH - 1))[0]
    return x, jnp.stack(new_re), jnp.stack(new_im), jnp.stack(new_s), jnp.stack(new_shift)


def _time_major(x):
    bsz, t_len, d = x.shape
    return jnp.transpose(x, (1, 0, 2)).reshape(t_len * bsz, d)


def _batch_major(x, bsz, t_len):
    return jnp.transpose(x.reshape(t_len, bsz, -1), (1, 0, 2))


def kernel(x_prompt, x_sample, mem_prompt, cache_mem_k, cache_mem_v, state_s5_re, state_s5_im, state_rwkv, state_shift, ln_g, ln_b, s5_a_re, s5_a_im, s5_log_dt, s5_b_re, s5_b_im, s5_c_re, s5_c_im, s5_d, s5_w_glu_v, s5_w_glu_g, rwkv_mu, rwkv_w_r, rwkv_w_k, rwkv_w_v, rwkv_w_o, rwkv_w0, rwkv_w1, rwkv_w2, rwkv_a0, rwkv_a1, rwkv_a2, rwkv_v0, rwkv_v1, rwkv_v2, rwkv_g1, rwkv_g2, rwkv_k_k, rwkv_k_a, rwkv_r_k, rwkv_lnx_g, rwkv_lnx_b, xa_w_q, xa_w_k, xa_w_v, xa_w_o, mlp_w1, mlp_w2):
    bf = lambda t: t.astype(BF16)
    n_s5, n_rwkv = state_s5_re.shape[0], state_rwkv.shape[0]
    b_p, t_p = x_prompt.shape[0], x_prompt.shape[1]
    b_s, t_s = x_sample.shape[0], x_sample.shape[1]

    head_id = jnp.arange(D_MODEL) // RWKV_HEAD
    ones_bd = (head_id[:, None] == head_id[None, :]).astype(BF16)
    w = {"ln_g": ln_g, "ln_b": ln_b, "s5": [], "rwkv": [],
         "xa_q": [bf(xa_w_q[i]) for i in range(DEPTH)], "xa_o": [bf(xa_w_o[i]) for i in range(DEPTH)],
         "mlp1": [bf(mlp_w1[i]) for i in range(DEPTH)], "mlp2": [bf(mlp_w2[i]) for i in range(DEPTH)]}
    for j in range(n_s5):
        a_re, a_im, wb, wc = _s5_prep(s5_a_re[j], s5_a_im[j], s5_log_dt[j], s5_b_re[j], s5_b_im[j],
                                      s5_c_re[j], s5_c_im[j])
        w["s5"].append((a_re, a_im, wb, wc, s5_d[j][None, :], bf(s5_w_glu_v[j]), bf(s5_w_glu_g[j])))
    for j in range(n_rwkv):
        v0 = rwkv_v0[j - 1] if j > 0 else jnp.zeros((D_MODEL,), F32)
        rw = {"mu": rwkv_mu[j], "w_r": bf(rwkv_w_r[j]), "w_k": bf(rwkv_w_k[j]), "w_v": bf(rwkv_w_v[j]),
              "w_o": bf(rwkv_w_o[j]), "w1": bf(rwkv_w1[j]), "w2": bf(rwkv_w2[j]), "a1": bf(rwkv_a1[j]),
              "a2": bf(rwkv_a2[j]), "g1": bf(rwkv_g1[j]), "g2": bf(rwkv_g2[j]), "ones": ones_bd,
              "proj_vec": jnp.stack([rwkv_w0[j], rwkv_a0[j], v0, rwkv_k_k[j], rwkv_k_a[j]]),
              "out_vec": jnp.stack([rwkv_r_k[j].reshape(D_MODEL), rwkv_lnx_g[j], rwkv_lnx_b[j]])}
        if j > 0:
            rw["v1"], rw["v2"] = bf(rwkv_v1[j - 1]), bf(rwkv_v2[j - 1])
        w["rwkv"].append(rw)

    mem_rows = mem_prompt.reshape(b_p * MEM_TOKENS, D_MODEL)
    mem4 = lambda t: t.reshape(DEPTH, b_p, MEM_TOKENS, D_MODEL)
    mk, mk16 = map(mem4, _mem_kv(mem_rows, xa_w_k, "mem_k"))
    mv, mv16 = map(mem4, _mem_kv(mem_rows, xa_w_v, "mem_v"))
    z_s5 = jnp.zeros((n_s5, b_p, S5_GROUPS, S5_STATE), F32)
    z_rwkv = jnp.zeros((n_rwkv, b_p, RWKV_HEADS, RWKV_HEAD, RWKV_HEAD), F32)
    z_shift = jnp.zeros((n_rwkv, b_p, D_MODEL), F32)
    assert b_p == SCAN_SEQS
    y_p, re_p, im_p, s_p, sh_p = _trunk(x_prompt, b_p, t_p, mk16, mv16, z_s5, z_s5, z_rwkv, z_shift, w)

    y_s, re_s, im_s, s_s, sh_s = _trunk(_time_major(x_sample), b_s, t_s, cache_mem_k, cache_mem_v,
                                        state_s5_re, state_s5_im, state_rwkv, state_shift, w)

    kv_shape = (DEPTH, b_p, MEM_TOKENS, XA_HEADS, XA_HEAD_DIM)
    return (y_p, _batch_major(y_s, b_s, t_s), mk.reshape(kv_shape), mv.reshape(kv_shape),
            re_p, im_p, s_p, sh_p, re_s, im_s, s_s, sh_s)
```

```python
import functools

import jax
import jax.numpy as jnp
from jax import lax
from jax.experimental import pallas as pl
from jax.experimental.pallas import tpu as pltpu

F32 = jnp.float32
BF16 = jnp.bfloat16

D_MODEL = 1024
DEPTH = 4
S5_GROUP = 16
S5_GROUPS = 64
S5_STATE = 64
S5_WIDTH = S5_GROUPS * S5_STATE
S5_BLOCKS = 4
S5_BLOCK_IN = D_MODEL // S5_BLOCKS
S5_BLOCK_ST = S5_WIDTH // S5_BLOCKS
RWKV_HEADS = 16
RWKV_HEAD = 64
RWKV_PAIRS = RWKV_HEADS // 2
PAIR_W = 2 * RWKV_HEAD
RWKV_CHUNK = 64
RWKV_GN_EPS = 64e-5
SCAN_SEQS = 8
SCAN_LANES = 512
SCAN_PAIRS = SCAN_LANES // PAIR_W
SCAN_GROUP = 4
SHORT_GROUPS = 2
MEM_TOKENS = 256
XA_HEADS = 4
XA_HEAD_DIM = 256
D_FF = 4 * D_MODEL
FF_CHUNK = 1024
LN_EPS = 1e-5
ALPHA = (2.0 * DEPTH) ** 0.25

V7X_VMEM_LIMIT = 56 * 1024 * 1024
ROW_TILE = 512
XA_ROWS = 1024


def _dot(a, b):
    return jnp.dot(a, b, preferred_element_type=F32)


def _dot_nt(a, b):
    return lax.dot_general(a, b, (((1,), (1,)), ((), ())), preferred_element_type=F32)


def _layer_norm(z, g, b):
    mu = jnp.mean(z, axis=-1, keepdims=True)
    zc = z - mu
    var = jnp.mean(zc * zc, axis=-1, keepdims=True)
    return zc * lax.rsqrt(var + LN_EPS) * g + b


def _to_seq_major(x):
    return jnp.transpose(x.reshape(x.shape[0] // 8, 8, x.shape[1]), (1, 0, 2))


def _to_time_major(x3):
    return jnp.transpose(x3, (1, 0, 2)).reshape(x3.shape[0] * x3.shape[1], x3.shape[2])


def _const_spec(shape):
    nd = len(shape)
    return pl.BlockSpec(tuple(shape), lambda *_: (0,) * nd, pipeline_mode=pl.Buffered(1))


def _params(*sem):
    return pltpu.CompilerParams(dimension_semantics=sem, vmem_limit_bytes=V7X_VMEM_LIMIT)


def _rows_call(body, name, rows, tm, row_ins, const_ins, outs, *, scratch=(), sequential=False,
               const_outs=(), seq_major_in=False, seq_major_out=False):
    seq_spec = lambda c: pl.BlockSpec((SCAN_SEQS, tm // SCAN_SEQS, c), lambda i: (0, i, 0))
    in_specs = [pl.BlockSpec((tm, a.shape[-1]), lambda i: (i, 0)) for a in row_ins]
    if seq_major_in:
        in_specs[0] = seq_spec(row_ins[0].shape[-1])
    in_specs += [_const_spec(a.shape) for a in const_ins]
    out_specs = [pl.BlockSpec((tm, c), lambda i: (i, 0)) for c, _ in outs]
    out_shape = [jax.ShapeDtypeStruct((rows, c), dt) for c, dt in outs]
    if seq_major_out:
        out_specs[0] = seq_spec(outs[0][0])
        out_shape[0] = jax.ShapeDtypeStruct((SCAN_SEQS, rows // SCAN_SEQS, outs[0][0]), outs[0][1])
    for shp, dt in const_outs:
        nd = len(shp)
        out_specs.append(pl.BlockSpec(tuple(shp), lambda i, nd=nd: (0,) * nd))
        out_shape.append(jax.ShapeDtypeStruct(tuple(shp), dt))
    return pl.pallas_call(
        body, grid=(rows // tm,), in_specs=in_specs, out_specs=out_specs, out_shape=out_shape,
        scratch_shapes=list(scratch),
        compiler_params=_params("arbitrary" if sequential else "parallel"), name=name,
    )(*row_ins, *const_ins)


def _matmul_body(x_ref, w_ref, o_ref):
    o_ref[...] = _dot(x_ref[...].astype(BF16), w_ref[...]).astype(o_ref.dtype)


def _matmul(x, w, name):
    rows = x.shape[0]
    return _rows_call(_matmul_body, name, rows, min(ROW_TILE, rows), [x], [w], [(w.shape[1], F32)])[0]


def _memkv_body(x_ref, w_ref, o_ref, o16_ref):
    y = _dot(x_ref[...].astype(BF16), w_ref[0].astype(BF16))
    o_ref[0] = y
    o16_ref[0] = y.astype(BF16)


def _mem_kv(mem_rows, w_stack, name):
    rows = mem_rows.shape[0]
    out_spec = pl.BlockSpec((1, rows, D_MODEL), lambda l: (l, 0, 0))
    return pl.pallas_call(
        _memkv_body, grid=(DEPTH,),
        in_specs=[_const_spec(mem_rows.shape), pl.BlockSpec((1, D_MODEL, D_MODEL), lambda l: (l, 0, 0))],
        out_specs=[out_spec, out_spec],
        out_shape=[jax.ShapeDtypeStruct((DEPTH, rows, D_MODEL), F32),
                   jax.ShapeDtypeStruct((DEPTH, rows, D_MODEL), BF16)],
        compiler_params=_params("parallel"), name=name,
    )(mem_rows, w_stack)


def _proj_ln_body(x_ref, o_ref, w_ref, g_ref, b_ref, out_ref):
    h = _dot(o_ref[...].astype(BF16), w_ref[...])
    out_ref[...] = _layer_norm(ALPHA * x_ref[...] + h, g_ref[...], b_ref[...])


def _mlp_body(x_ref, w1_ref, w2_ref, g_ref, b_ref, out_ref):
    x = x_ref[...]
    xb = x.astype(BF16)
    acc = None
    for c in range(D_FF // FF_CHUNK):
        h = _dot(xb, w1_ref[:, c * FF_CHUNK:(c + 1) * FF_CHUNK])
        h = jnp.maximum(h, 0.0)
        p = _dot((h * h).astype(BF16), w2_ref[c * FF_CHUNK:(c + 1) * FF_CHUNK, :])
        acc = p if acc is None else acc + p
    res = _layer_norm(ALPHA * x + acc, g_ref[...], b_ref[...])
    out_ref[...] = res if len(out_ref.shape) == 2 else _to_seq_major(res)


def _softmax_rows(s):
    m = jnp.max(s, axis=-1, keepdims=True)
    e = jnp.exp(s - m)
    return e / jnp.sum(e, axis=-1, keepdims=True)


def _xattn_fused_body(x_ref, wq_ref, k_ref, v_ref, wo_ref, g_ref, b_ref, out_ref):
    x = x_ref[...]
    q3 = _to_seq_major(_dot(x.astype(BF16), wq_ref[...])).astype(BF16)
    heads = [slice(h * XA_HEAD_DIM, (h + 1) * XA_HEAD_DIM) for h in range(XA_HEADS)]
    units = [(b, sl) for b in range(SCAN_SEQS) for sl in heads]
    s = [_dot_nt(q3[b][:, sl], k_ref[b][:, sl]) * (XA_HEAD_DIM ** -0.5) for b, sl in units]
    p = [_softmax_rows(t).astype(BF16) for t in s]
    pv = [_dot(p[u], v_ref[b][:, sl]) for u, (b, sl) in enumerate(units)]
    per_seq = [jnp.concatenate(pv[b * XA_HEADS:(b + 1) * XA_HEADS], axis=-1) for b in range(SCAN_SEQS)]
    o = _to_time_major(jnp.stack(per_seq, axis=0))
    out_ref[...] = _layer_norm(ALPHA * x + _dot(o.astype(BF16), wo_ref[...]), g_ref[...], b_ref[...])


def _xattn_fused(x, wq, mem_k, mem_v, wo, ln_g, ln_b, layer):
    rows = x.shape[0]
    tm = min(XA_ROWS, rows)
    tile = pl.BlockSpec((tm, D_MODEL), lambda i: (i, 0))
    kv_spec = pl.BlockSpec((None, SCAN_SEQS, MEM_TOKENS, D_MODEL), lambda i: (layer, 0, 0, 0),
                           pipeline_mode=pl.Buffered(1))
    vec = _const_spec((1, D_MODEL))
    return pl.pallas_call(
        _xattn_fused_body, grid=(rows // tm,),
        in_specs=[tile, _const_spec(wq.shape), kv_spec, kv_spec, _const_spec(wo.shape), vec, vec],
        out_specs=tile, out_shape=jax.ShapeDtypeStruct((rows, D_MODEL), F32),
        compiler_params=_params("parallel"), name="xattn_fused",
    )(x, wq, mem_k, mem_v, wo, ln_g, ln_b)


def _attn_cache_body(q_ref, k_ref, v_ref, o_ref):
    t_len, n_seq = q_ref.shape[0], q_ref.shape[1]
    shape = (XA_HEADS * t_len, MEM_TOKENS * XA_HEADS)
    own_head = (lax.broadcasted_iota(jnp.int32, shape, 0) // t_len
                == lax.broadcasted_iota(jnp.int32, shape, 1) % XA_HEADS)
    q3 = _to_seq_major(q_ref[...].reshape(t_len * n_seq, D_MODEL))
    outs = []
    for b in range(n_seq):
        q = q3[b]
        qs = jnp.concatenate([q[:, h * XA_HEAD_DIM:(h + 1) * XA_HEAD_DIM] for h in range(XA_HEADS)], axis=0)
        k2 = k_ref[b].reshape(MEM_TOKENS * XA_HEADS, XA_HEAD_DIM).astype(BF16)
        v2 = v_ref[b].reshape(MEM_TOKENS * XA_HEADS, XA_HEAD_DIM).astype(BF16)
        s = _dot_nt(qs.astype(BF16), k2) * (XA_HEAD_DIM ** -0.5)
        p = _softmax_rows(jnp.where(own_head, s, -1e30))
        o = _dot(p.astype(BF16), v2)
        outs.append(jnp.concatenate([o[h * t_len:(h + 1) * t_len] for h in range(XA_HEADS)], axis=-1))
    o_ref[...] = _to_time_major(jnp.stack(outs, axis=0)).reshape(t_len, n_seq, D_MODEL)


def _attention_cache(q_rows, cache_k, cache_v, layer, batch, t_len):
    nb = SCAN_SEQS
    assert t_len == 8 and batch % nb == 0
    kv_spec = pl.BlockSpec((None, nb, MEM_TOKENS, XA_HEADS, XA_HEAD_DIM), lambda b: (layer, b, 0, 0, 0))
    tile = pl.BlockSpec((t_len, nb, D_MODEL), lambda b: (0, b, 0))
    return pl.pallas_call(
        _attn_cache_body, grid=(batch // nb,), in_specs=[tile, kv_spec, kv_spec], out_specs=tile,
        out_shape=jax.ShapeDtypeStruct((t_len, batch, D_MODEL), F32),
        compiler_params=_params("parallel"), name="xattn_cache",
    )(q_rows.reshape(t_len, batch, D_MODEL), cache_k, cache_v).reshape(t_len * batch, D_MODEL)


def _s5_prep_body(lre_ref, lim_ref, ldt_ref, br_ref, bi_ref, are_ref, aim_ref, bbre_ref, bbim_ref):
    lre = lre_ref[...]
    lim = lim_ref[...]
    dt = jnp.exp(ldt_ref[...])
    mag = jnp.exp(lre * dt)
    ang = lim * dt
    ab_re = mag * jnp.cos(ang)
    ab_im = mag * jnp.sin(ang)
    den = lre * lre + lim * lim
    num_re = ab_re - 1.0
    f_re = (num_re * lre + ab_im * lim) / den
    f_im = (ab_im * lre - num_re * lim) / den
    br = br_ref[...]
    bi = bi_ref[...]
    are_ref[...] = ab_re
    aim_ref[...] = ab_im
    bbre_ref[...] = f_re * br - f_im * bi
    bbim_ref[...] = f_re * bi + f_im * br


def _s5_prep(a_re, a_im, log_dt, b_re, b_im, c_re, c_im):
    rep = lambda t: jnp.repeat(t, S5_GROUP, axis=0)
    to_rows = lambda t: jnp.transpose(t, (0, 2, 1)).reshape(S5_GROUPS * S5_GROUP, S5_STATE)
    ins = [rep(a_re), rep(a_im), rep(jnp.broadcast_to(log_dt[:, None], (S5_GROUPS, S5_STATE))),
           to_rows(b_re), to_rows(b_im)]
    shp = jax.ShapeDtypeStruct((S5_GROUPS * S5_GROUP, S5_STATE), F32)
    full = pl.BlockSpec((S5_GROUPS * S5_GROUP, S5_STATE), lambda: (0, 0))
    are, aim, bbre, bbim = pl.pallas_call(
        _s5_prep_body, in_specs=[full] * 5, out_specs=[full] * 4, out_shape=[shp] * 4, name="s5_prep",
    )(*ins)
    a_row = lambda t: t[::S5_GROUP].reshape(1, S5_WIDTH)
    eye = jnp.eye(S5_GROUP, dtype=F32)
    blk_b = lambda t: jnp.einsum('qgcn,gh->qgchn', t.reshape(S5_BLOCKS, 16, S5_GROUP, S5_STATE), eye
                                 ).reshape(S5_BLOCKS, S5_BLOCK_IN, S5_BLOCK_ST)
    wb = jnp.concatenate([blk_b(bbre), blk_b(bbim)], axis=2).astype(BF16)
    blk_c = lambda t: jnp.einsum('qgcn,gh->qhngc', t.reshape(S5_BLOCKS, 16, S5_GROUP, S5_STATE), eye
                                 ).reshape(S5_BLOCKS, S5_BLOCK_ST, S5_BLOCK_IN)
    wc = jnp.concatenate([blk_c(c_re), -blk_c(c_im)], axis=1).astype(BF16)
    return a_row(are), a_row(aim), wb, wc


def _s5_body(batch, x_ref, h0re_ref, h0im_ref, are_ref, aim_ref, wb_ref, wc_ref, d_ref, wv_ref, wg_ref,
             g_ref, b_ref, out_ref, hre_ref, him_ref, hbuf, sre, sim):
    x = x_ref[...] if len(x_ref.shape) == 2 else _to_time_major(x_ref[...])
    rows = x.shape[0]
    steps = rows // batch

    @pl.when(pl.program_id(0) == 0)
    def _():
        sre[...] = h0re_ref[...]
        sim[...] = h0im_ref[...]

    xb = x.astype(BF16)
    for q in range(S5_BLOCKS):
        hbuf[:, q * 2 * S5_BLOCK_ST:(q + 1) * 2 * S5_BLOCK_ST] = _dot(
            xb[:, q * S5_BLOCK_IN:(q + 1) * S5_BLOCK_IN], wb_ref[q])

    def row_group(rg, carry):
        r_off = pl.multiple_of(rg * 8, 8)
        for q in range(S5_BLOCKS):
            st = slice(q * S5_BLOCK_ST, (q + 1) * S5_BLOCK_ST)
            c_re = slice(q * 2 * S5_BLOCK_ST, q * 2 * S5_BLOCK_ST + S5_BLOCK_ST)
            c_im = slice(q * 2 * S5_BLOCK_ST + S5_BLOCK_ST, (q + 1) * 2 * S5_BLOCK_ST)
            ar = jnp.broadcast_to(are_ref[:, st], (8, S5_BLOCK_ST))
            ai = jnp.broadcast_to(aim_ref[:, st], (8, S5_BLOCK_ST))

            def step(t, h):
                hr, hi = h
                row = pl.multiple_of(t * batch + r_off, 8)
                nr = ar * hr - ai * hi + hbuf[pl.ds(row, 8), c_re]
                ni = ar * hi + ai * hr + hbuf[pl.ds(row, 8), c_im]
                hbuf[pl.ds(row, 8), c_re] = nr
                hbuf[pl.ds(row, 8), c_im] = ni
                return nr, ni

            hr, hi = lax.fori_loop(0, steps, step, (sre[pl.ds(r_off, 8), st], sim[pl.ds(r_off, 8), st]),
                                   unroll=4)
            sre[pl.ds(r_off, 8), st] = hr
            sim[pl.ds(r_off, 8), st] = hi
        return carry

    lax.fori_loop(0, batch // 8, row_group, 0)
    hre_ref[...] = sre[...]
    him_ref[...] = sim[...]

    ys = [_dot(hbuf[:, q * 2 * S5_BLOCK_ST:(q + 1) * 2 * S5_BLOCK_ST].astype(BF16), wc_ref[q])
          for q in range(S5_BLOCKS)]
    y = jnp.concatenate(ys, axis=-1) + d_ref[...] * x
    z = jax.nn.gelu(y).astype(BF16)
    h = _dot(z, wv_ref[...]) * jax.nn.sigmoid(_dot(z, wg_ref[...]))
    out_ref[...] = _layer_norm(ALPHA * x + h, g_ref[...], b_ref[...])


def _s5_layer(x, batch, h0re, h0im, s5w, ln_g, ln_b):
    rows = x.size // D_MODEL
    tm = min(ROW_TILE, rows)
    a_re, a_im, wb, wc, d_vec, wv, wg = s5w
    out, hre, him = _rows_call(
        functools.partial(_s5_body, batch), "s5_mixer", rows, tm, [x],
        [h0re, h0im, a_re, a_im, wb, wc, d_vec, wv, wg, ln_g, ln_b], [(D_MODEL, F32)],
        scratch=[pltpu.VMEM((tm, 2 * S5_WIDTH), F32), pltpu.VMEM((batch, S5_WIDTH), F32),
                 pltpu.VMEM((batch, S5_WIDTH), F32)],
        sequential=True, const_outs=[((batch, S5_WIDTH), F32)] * 2, seq_major_in=(x.ndim == 3))
    return out, hre, him


def _head_sum(v, ones_ref):
    return _dot(v.astype(BF16), ones_ref[...])


def _token_shift_mix(batch, x_ref, shift_ref, mu_ref, xprev):
    rows = x_ref.shape[0]

    @pl.when(pl.program_id(0) == 0)
    def _():
        xprev[...] = shift_ref[...]

    x = x_ref[...]
    if rows > batch:
        xs = jnp.concatenate([xprev[...], x[:rows - batch]], axis=0)
    else:
        xs = xprev[...]
    xprev[...] = x[rows - batch:]
    xx = xs - x
    return lambda i: (x + xx * mu_ref[i:i + 1, :]).astype(BF16)


def _rwkv_proj_rk_body(batch, x_ref, shift_ref, mu_ref, wr_ref, wk_ref, a1_ref, a2_ref, vec_ref, ones_ref,
                       r_out, k_out, a_out, b_out, xprev):
    mix = _token_shift_mix(batch, x_ref, shift_ref, mu_ref, xprev)
    a0, k_k, k_a = vec_ref[1:2, :], vec_ref[3:4, :], vec_ref[4:5, :]
    r_out[...] = _dot(mix(0), wr_ref[...])
    k = _dot(mix(2), wk_ref[...])
    a = jax.nn.sigmoid(a0 + _dot(_dot(mix(4), a1_ref[...]).astype(BF16), a2_ref[...]))
    kk = k * k_k
    sq = kk * kk
    sq_hi = sq.astype(BF16)
    ss = _dot(sq_hi, ones_ref[...]) + _head_sum(sq - sq_hi.astype(F32), ones_ref)
    kk = kk / jnp.maximum(jnp.sqrt(ss), 1e-12)
    k_out[...] = k * (1.0 + (a - 1.0) * k_a)
    a_out[...] = -kk
    b_out[...] = kk * a


def _rwkv_proj_vwg_body(batch, has_vgate, *refs):
    if has_vgate:
        (x_ref, vf_ref, shift_ref, mu_ref, wv_ref, w1_ref, w2_ref, g1_ref, g2_ref, v1_ref, v2_ref, vec_ref,
         lw_out, v_out, g_out, xprev) = refs
    else:
        (x_ref, shift_ref, mu_ref, wv_ref, w1_ref, w2_ref, g1_ref, g2_ref, vec_ref,
         lw_out, v_out, g_out, xprev) = refs
    mix = _token_shift_mix(batch, x_ref, shift_ref, mu_ref, xprev)
    w0, v0 = vec_ref[0:1, :], vec_ref[2:3, :]
    xv = mix(3)
    v = _dot(xv, wv_ref[...])
    w_lora = _dot(jnp.tanh(_dot(mix(1), w1_ref[...])).astype(BF16), w2_ref[...])
    w_log = -jax.nn.softplus(-(w0 + w_lora)) - 0.5
    lw_out[...] = -jnp.exp(w_log)
    if has_vgate:
        gate_v = jax.nn.sigmoid(v0 + _dot(_dot(xv, v1_ref[...]).astype(BF16), v2_ref[...]))
        v = v + (vf_ref[...] - v) * gate_v
    v_out[...] = v
    g_out[...] = _dot(jax.nn.sigmoid(_dot(mix(5), g1_ref[...])).astype(BF16), g2_ref[...])


def _rwkv_consts(seq_len=RWKV_CHUNK):
    L = RWKV_CHUNK
    lane = lax.broadcasted_iota(jnp.int32, (1, PAIR_W), 1)
    t_idx = lax.broadcasted_iota(jnp.int32, (L, PAIR_W), 0)
    s_idx = lax.broadcasted_iota(jnp.int32, (L, PAIR_W), 1) % L
    r2 = lax.broadcasted_iota(jnp.int32, (2 * L, PAIR_W), 0)
    c2 = lax.broadcasted_iota(jnp.int32, (2 * L, PAIR_W), 1)
    rr = lax.broadcasted_iota(jnp.int32, (L, L), 0)
    cc = lax.broadcasted_iota(jnp.int32, (L, L), 1)
    strict_lower, incl_lower, tri = s_idx < t_idx, s_idx <= t_idx, cc <= rr
    cst = {}
    if seq_len < L:
        same_seq = (s_idx // seq_len) == (t_idx // seq_len)
        strict_lower = same_seq & strict_lower
        incl_lower = same_seq & incl_lower
        tri = ((cc // seq_len) == (rr // seq_len)) & tri
        cst["seq_ones"] = jnp.where((cc // seq_len) == (rr // seq_len), 1.0, 0.0).astype(BF16)
        cst["col_seq"] = (c2 % L) // seq_len
    cst.update(
        m0=lane < RWKV_HEAD,
        strict_lower=strict_lower, incl_lower=incl_lower,
        same_head=(r2 < L) == (c2 < RWKV_HEAD),
        eye2=jnp.where(r2 == c2, 1.0, 0.0).astype(F32),
        tri=jnp.where(tri, 1.0, 0.0).astype(BF16),
        zeros_l=jnp.zeros((L, PAIR_W), F32))
    return cst


def _expand_state(sc, m0):
    return jnp.concatenate([jnp.where(m0, sc, 0.0), jnp.where(m0, 0.0, sc)], axis=0)


def _compact_state(s_bd):
    return s_bd[0:RWKV_HEAD] + s_bd[RWKV_HEAD:PAIR_W]


def _rwkv_chunk(slabs, states, cst, seq_len=RWKV_CHUNK):
    L = RWKV_CHUNK
    n_seq = L // seq_len
    m0 = cst["m0"]
    bf = lambda t: t.astype(BF16)
    cat = lambda ts: jnp.concatenate(ts, axis=0)
    ap, rp, vp, btp, ktp, wbp, wkp, dtot = [], [], [], [], [], [], [], []
    for r, lw, k, v, a, b in slabs:
        h1 = bf(lw)
        e1 = lw - h1.astype(F32)
        h2 = bf(e1)
        h3 = bf(e1 - h2.astype(F32))
        cum = _dot(cst["tri"], h1) + _dot(cst["tri"], h2) + _dot(cst["tri"], h3)
        if n_seq == 1:
            tot = cum[L - 1:L, :]
        else:
            tot = _dot(cst["seq_ones"], h1) + _dot(cst["seq_ones"], h2) + _dot(cst["seq_ones"], h3)
        inv = jnp.exp(-cum)
        tail = jnp.exp(tot - cum)
        pieces = (a * jnp.exp(cum - lw), r * jnp.exp(cum), v, b * inv, k * inv, b * tail, k * tail,
                  jnp.exp(tot))
        for p in range(r.shape[1] // PAIR_W):
            sl = slice(p * PAIR_W, (p + 1) * PAIR_W)
            for dst, src in zip((ap, rp, vp, btp, ktp, wbp, wkp, dtot), pieces):
                dst.append(src[:, sl])
    P = range(len(ap))
    gram = [_dot_nt(bf(cat([ap[p], rp[p]])),
                    bf(cat([jnp.where(m0, btp[p], 0.0), jnp.where(m0, ktp[p], 0.0),
                            jnp.where(m0, 0.0, btp[p]), jnp.where(m0, 0.0, ktp[p])]))) for p in P]
    if n_seq == 1:
        ar_s = [_dot_nt(bf(cat([ap[p], rp[p]])), bf(states[p])) for p in P]
    else:
        ar_s = []
        for p in P:
            parts = [_dot_nt(bf(cat([ap[p][s * seq_len:(s + 1) * seq_len], rp[p][s * seq_len:(s + 1) * seq_len]])),
                             bf(states[p * n_seq + s])) for s in range(n_seq)]
            ar_s.append(cat([o[0:seq_len] for o in parts] + [o[seq_len:2 * seq_len] for o in parts]))
    na0 = [jnp.where(cst["strict_lower"], gram[p][0:L, 0:PAIR_W], 0.0) for p in P]
    na1 = [jnp.where(cst["strict_lower"], gram[p][0:L, PAIR_W:2 * PAIR_W], 0.0) for p in P]
    nr = [cat([jnp.where(cst["incl_lower"], gram[p][L:2 * L, 0:PAIR_W], 0.0),
               jnp.where(cst["incl_lower"], gram[p][L:2 * L, PAIR_W:2 * PAIR_W], 0.0)]) for p in P]
    nv = [_dot(bf(cat([na0[p], na1[p]])), bf(cat([cst["zeros_l"], vp[p]]))) for p in P]
    pw = [cat([jnp.where(m0, na0[p], 0.0), jnp.where(m0, 0.0, pltpu.roll(na1[p], RWKV_HEAD, axis=1))])
          for p in P]
    t_inv = [cst["eye2"] + pw[p] for p in P]
    for _ in range(seq_len.bit_length() - 2):
        pwb = [bf(pw[p]) for p in P]
        pw = [_dot(pwb[p], pwb[p]) for p in P]
        t_inv = [t_inv[p] + _dot(bf(t_inv[p]), bf(pw[p])) for p in P]
    u_st = [_dot(bf(t_inv[p]), bf(cat([ar_s[p][0:L], ar_s[p][0:L]]) + nv[p])) for p in P]
    zz = [cat([jnp.where(m0, u_st[p][0:L], u_st[p][L:2 * L]), vp[p]]) for p in P]
    y_st = [_dot(bf(nr[p]), bf(zz[p])) for p in P]
    ys = [ar_s[p][L:2 * L] + jnp.where(m0, y_st[p][0:L], y_st[p][L:2 * L]) for p in P]
    if n_seq == 1:
        upd = [_dot(bf(zz[p].T), bf(cat([wbp[p], wkp[p]]))) for p in P]
        new_states = [states[p] * dtot[p] + jnp.where(cst["same_head"], upd[p], 0.0) for p in P]
    else:
        new_states = []
        for p in P:
            zz_t = zz[p].T
            lhs = cat([jnp.where(cst["col_seq"] == s, zz_t, 0.0) for s in range(n_seq)])
            upd = _dot(bf(lhs), bf(cat([wbp[p], wkp[p]])))
            for s in range(n_seq):
                new_states.append(states[p * n_seq + s] * dtot[p][s * seq_len:s * seq_len + 1]
                                  + jnp.where(cst["same_head"], upd[s * PAIR_W:(s + 1) * PAIR_W], 0.0))
    y_slabs, at = [], 0
    for slab in slabs:
        n_i = slab[0].shape[1] // PAIR_W
        y_slabs.append(jnp.concatenate(ys[at:at + n_i], axis=1) if n_i > 1 else ys[at])
        at += n_i
    return y_slabs, new_states


def _rwkv_scan_short_body(r_ref, lw_ref, k_ref, v_ref, a_ref, b_ref, s0_ref, y_ref, sout_ref):
    t_len, n_seq = r_ref.shape[0], SCAN_SEQS
    cst = _rwkv_consts(t_len)
    slabs, states = [], []
    for g in range(SHORT_GROUPS):
        rows = lambda ref: _to_seq_major(ref[:, g * n_seq:(g + 1) * n_seq, :].reshape(t_len * n_seq, D_MODEL)
                                         ).reshape(n_seq * t_len, D_MODEL)
        slabs.append(tuple(rows(ref) for ref in (r_ref, lw_ref, k_ref, v_ref, a_ref, b_ref)))
        states += [_expand_state(s0_ref[g * n_seq + s, p], cst["m0"])
                   for p in range(RWKV_PAIRS) for s in range(n_seq)]
    ys, new_states = _rwkv_chunk(slabs, states, cst, t_len)
    for g in range(SHORT_GROUPS):
        y_ref[:, g * n_seq:(g + 1) * n_seq, :] = _to_time_major(
            ys[g].reshape(n_seq, t_len, D_MODEL)).reshape(t_len, n_seq, D_MODEL)
        for s in range(n_seq):
            for p in range(RWKV_PAIRS):
                sout_ref[g * n_seq + s, p] = _compact_state(new_states[(g * RWKV_PAIRS + p) * n_seq + s])


def _rwkv_scan_tm_body(r_ref, lw_ref, k_ref, v_ref, a_ref, b_ref, s0_ref, y_ref, sout_ref, bufs, ybuf, s_scr):
    cst = _rwkv_consts()

    @pl.when(pl.program_id(1) == 0)
    def _():
        for b in range(SCAN_SEQS):
            for p in range(SCAN_PAIRS):
                s_scr[b * SCAN_PAIRS + p] = _expand_state(s0_ref[b, p], cst["m0"])

    for i, ref in enumerate((r_ref, lw_ref, k_ref, v_ref, a_ref, b_ref)):
        bufs[i] = _to_seq_major(ref[...])

    def sequence_group(i, carry):
        seqs = tuple(SCAN_GROUP * i + n for n in range(SCAN_GROUP))
        slabs = [tuple(bufs[j, s] for j in range(6)) for s in seqs]
        states = [s_scr[s * SCAN_PAIRS + p] for s in seqs for p in range(SCAN_PAIRS)]
        y_slabs, new_states = _rwkv_chunk(slabs, states, cst)
        for n, s in enumerate(seqs):
            ybuf[s] = y_slabs[n]
            for p in range(SCAN_PAIRS):
                s_scr[s * SCAN_PAIRS + p] = new_states[n * SCAN_PAIRS + p]
        return carry

    lax.fori_loop(0, SCAN_SEQS // SCAN_GROUP, sequence_group, 0)
    y_ref[...] = _to_time_major(ybuf[...])

    @pl.when(pl.program_id(1) == pl.num_programs(1) - 1)
    def _():
        for b in range(SCAN_SEQS):
            for p in range(SCAN_PAIRS):
                sout_ref[b, p] = _compact_state(s_scr[b * SCAN_PAIRS + p])


def _rwkv_out_body(y_ref, r_ref, k_ref, v_ref, g_ref, x_ref, ones_ref, vec_ref, wo_ref, lg_ref, lb_ref, out_ref):
    y = y_ref[...]
    r_k, lnx_g, lnx_b = (vec_ref[i:i + 1, :] for i in range(3))
    inv_n = 1.0 / RWKV_HEAD
    mu = _head_sum(y, ones_ref) * inv_n
    yc = y - mu
    var = _head_sum(yc * yc, ones_ref) * inv_n
    yn = yc * lax.rsqrt(var + RWKV_GN_EPS) * lnx_g + lnx_b
    bonus = _head_sum(r_ref[...] * k_ref[...] * r_k, ones_ref) * v_ref[...]
    o = ((yn + bonus) * g_ref[...]).astype(BF16)
    out_ref[...] = _layer_norm(ALPHA * x_ref[...] + _dot(o, wo_ref[...]), lg_ref[...], lb_ref[...])


def _pair_state(s):
    bsz = s.shape[0]
    s = s.reshape(bsz, RWKV_PAIRS, 2, RWKV_HEAD, RWKV_HEAD)
    return jnp.transpose(s, (0, 1, 3, 2, 4)).reshape(bsz, RWKV_PAIRS, RWKV_HEAD, PAIR_W)


def _unpair_state(s):
    bsz = s.shape[0]
    s = s.reshape(bsz, RWKV_PAIRS, RWKV_HEAD, 2, RWKV_HEAD)
    return jnp.transpose(s, (0, 1, 3, 2, 4)).reshape(bsz, RWKV_HEADS, RWKV_HEAD, RWKV_HEAD)


def _rwkv_layer(x, batch, t_len, s0, shift0, v_first, rw, ln_g, ln_b):
    rows = x.shape[0]
    tm = min(ROW_TILE, rows)
    has_vgate = v_first is not None
    shift_scratch = [pltpu.VMEM((batch, D_MODEL), F32)]
    r, k, a, b = _rows_call(
        functools.partial(_rwkv_proj_rk_body, batch), "rwkv_proj_rk", rows, tm, [x],
        [shift0, rw["mu"], rw["w_r"], rw["w_k"], rw["a1"], rw["a2"], rw["proj_vec"], rw["ones"]],
        [(D_MODEL, F32)] * 4, scratch=shift_scratch, sequential=True)
    consts = [shift0, rw["mu"], rw["w_v"], rw["w1"], rw["w2"], rw["g1"], rw["g2"]]
    if has_vgate:
        consts += [rw["v1"], rw["v2"]]
    lw, v, g = _rows_call(
        functools.partial(_rwkv_proj_vwg_body, batch, has_vgate), "rwkv_proj_vwg", rows, tm,
        [x] + ([v_first] if has_vgate else []), consts + [rw["proj_vec"]],
        [(D_MODEL, F32)] * 3, scratch=shift_scratch, sequential=True)

    if batch == SCAN_SEQS:
        assert t_len % RWKV_CHUNK == 0
        n_half = D_MODEL // SCAN_LANES
        tile = pl.BlockSpec((SCAN_SEQS * RWKV_CHUNK, SCAN_LANES), lambda h, c: (c, h))
        st_spec = pl.BlockSpec((batch, None, SCAN_PAIRS, RWKV_HEAD, PAIR_W), lambda h, c: (0, h, 0, 0, 0))
        st_shape = (batch, n_half, SCAN_PAIRS, RWKV_HEAD, PAIR_W)
        y, s_last = pl.pallas_call(
            _rwkv_scan_tm_body, grid=(n_half, t_len // RWKV_CHUNK),
            in_specs=[tile] * 6 + [st_spec], out_specs=[tile, st_spec],
            out_shape=[jax.ShapeDtypeStruct((rows, D_MODEL), F32), jax.ShapeDtypeStruct(st_shape, F32)],
            scratch_shapes=[pltpu.VMEM((6, SCAN_SEQS, RWKV_CHUNK, SCAN_LANES), F32),
                            pltpu.VMEM((SCAN_SEQS, RWKV_CHUNK, SCAN_LANES), F32),
                            pltpu.VMEM((SCAN_SEQS * SCAN_PAIRS, PAIR_W, PAIR_W), F32)],
            compiler_params=_params("arbitrary", "arbitrary"), name="rwkv_scan",
        )(r, lw, k, v, a, b, _pair_state(s0).reshape(st_shape))
        s_last = s_last.reshape(batch, RWKV_PAIRS, RWKV_HEAD, PAIR_W)
    else:
        n_seq = SCAN_SEQS * SHORT_GROUPS
        assert t_len * SCAN_SEQS == RWKV_CHUNK and batch % n_seq == 0
        view = lambda t: t.reshape(t_len, batch, D_MODEL)
        tile = pl.BlockSpec((t_len, n_seq, D_MODEL), lambda bi: (0, bi, 0))
        st_spec = pl.BlockSpec((n_seq, RWKV_PAIRS, RWKV_HEAD, PAIR_W), lambda bi: (bi, 0, 0, 0))
        y2, s_last = pl.pallas_call(
            _rwkv_scan_short_body, grid=(batch // n_seq,),
            in_specs=[tile] * 6 + [st_spec], out_specs=[tile, st_spec],
            out_shape=[jax.ShapeDtypeStruct((t_len, batch, D_MODEL), F32),
                       jax.ShapeDtypeStruct((batch, RWKV_PAIRS, RWKV_HEAD, PAIR_W), F32)],
            compiler_params=_params("parallel"), name="rwkv_scan_short",
        )(view(r), view(lw), view(k), view(v), view(a), view(b), _pair_state(s0))
        y = y2.reshape(rows, D_MODEL)

    out = _rows_call(_rwkv_out_body, "rwkv_out", rows, tm, [y, r, k, v, g, x],
                     [rw["ones"], rw["out_vec"], rw["w_o"], ln_g, ln_b], [(D_MODEL, F32)])[0]
    return out, _unpair_state(s_last), x[rows - batch:], (v if not has_vgate else v_first)


def _trunk(x, batch, t_len, mem_k, mem_v, s5_re, s5_im, rwkv_s, shift, w):
    rows = x.size // D_MODEL
    tm = min(ROW_TILE, rows)
    new_re, new_im, new_s, new_shift = [], [], [], []
    v_first = None
    for i in range(DEPTH):
        j = i // 2
        ln = lambda n: (w["ln_g"][i, n][None, :], w["ln_b"][i, n][None, :])
        if i % 2 == 0:
            x, hre, him = _s5_layer(x, batch, s5_re[j].reshape(batch, S5_WIDTH),
                                    s5_im[j].reshape(batch, S5_WIDTH), w["s5"][j], *ln(0))
            new_re.append(hre.reshape(batch, S5_GROUPS, S5_STATE))
            new_im.append(him.reshape(batch, S5_GROUPS, S5_STATE))
        else:
            x, s_last, x_last, v_first = _rwkv_layer(x, batch, t_len, rwkv_s[j], shift[j], v_first,
                                                     w["rwkv"][j], *ln(0))
            new_s.append(s_last)
            new_shift.append(x_last)
        if batch == SCAN_SEQS:
            x = _xattn_fused(x, w["xa_q"][i], mem_k, mem_v, w["xa_o"][i], *ln(1), i)
        else:
            o = _attention_cache(_matmul(x, w["xa_q"][i], "xattn_q"), mem_k, mem_v, i, batch, t_len)
            x = _rows_call(_proj_ln_body, "xattn_out", rows, tm, [x, o],
                           [w["xa_o"][i], *ln(1)], [(D_MODEL, F32)])[0]
        x = _rows_call(_mlp_body, "mlp", rows, tm, [x], [w["mlp1"][i], w["mlp2"][i], *ln(2)], [(D_MODEL, F32)],
                       seq_major_out=(batch == SCAN_SEQS and i == DEPTH - 1))[0]
    return x, jnp.stack(new_re), jnp.stack(new_im), jnp.stack(new_s), jnp.stack(new_shift)


def _time_major(x):
    bsz, t_len, d = x.shape
    return jnp.transpose(x, (1, 0, 2)).reshape(t_len * bsz, d)


def _batch_major(x, bsz, t_len):
    return jnp.transpose(x.reshape(t_len, bsz, -1), (1, 0, 2))


def kernel(x_prompt, x_sample, mem_prompt, cache_mem_k, cache_mem_v, state_s5_re, state_s5_im, state_rwkv, state_shift, ln_g, ln_b, s5_a_re, s5_a_im, s5_log_dt, s5_b_re, s5_b_im, s5_c_re, s5_c_im, s5_d, s5_w_glu_v, s5_w_glu_g, rwkv_mu, rwkv_w_r, rwkv_w_k, rwkv_w_v, rwkv_w_o, rwkv_w0, rwkv_w1, rwkv_w2, rwkv_a0, rwkv_a1, rwkv_a2, rwkv_v0, rwkv_v1, rwkv_v2, rwkv_g1, rwkv_g2, rwkv_k_k, rwkv_k_a, rwkv_r_k, rwkv_lnx_g, rwkv_lnx_b, xa_w_q, xa_w_k, xa_w_v, xa_w_o, mlp_w1, mlp_w2):
    bf = lambda t: t.astype(BF16)
    n_s5, n_rwkv = state_s5_re.shape[0], state_rwkv.shape[0]
    b_p, t_p = x_prompt.shape[0], x_prompt.shape[1]
    b_s, t_s = x_sample.shape[0], x_sample.shape[1]

    head_id = jnp.arange(D_MODEL) // RWKV_HEAD
    ones_bd = (head_id[:, None] == head_id[None, :]).astype(BF16)
    w = {"ln_g": ln_g, "ln_b": ln_b, "s5": [], "rwkv": [],
         "xa_q": [bf(xa_w_q[i]) for i in range(DEPTH)], "xa_o": [bf(xa_w_o[i]) for i in range(DEPTH)],
         "mlp1": [bf(mlp_w1[i]) for i in range(DEPTH)], "mlp2": [bf(mlp_w2[i]) for i in range(DEPTH)]}
    for j in range(n_s5):
        a_re, a_im, wb, wc = _s5_prep(s5_a_re[j], s5_a_im[j], s5_log_dt[j], s5_b_re[j], s5_b_im[j],
                                      s5_c_re[j], s5_c_im[j])
        w["s5"].append((a_re, a_im, wb, wc, s5_d[j][None, :], bf(s5_w_glu_v[j]), bf(s5_w_glu_g[j])))
    for j in range(n_rwkv):
        v0 = rwkv_v0[j - 1] if j > 0 else jnp.zeros((D_MODEL,), F32)
        rw = {"mu": rwkv_mu[j], "w_r": bf(rwkv_w_r[j]), "w_k": bf(rwkv_w_k[j]), "w_v": bf(rwkv_w_v[j]),
              "w_o": bf(rwkv_w_o[j]), "w1": bf(rwkv_w1[j]), "w2": bf(rwkv_w2[j]), "a1": bf(rwkv_a1[j]),
              "a2": bf(rwkv_a2[j]), "g1": bf(rwkv_g1[j]), "g2": bf(rwkv_g2[j]), "ones": ones_bd,
              "proj_vec": jnp.stack([rwkv_w0[j], rwkv_a0[j], v0, rwkv_k_k[j], rwkv_k_a[j]]),
              "out_vec": jnp.stack([rwkv_r_k[j].reshape(D_MODEL), rwkv_lnx_g[j], rwkv_lnx_b[j]])}
        if j > 0:
            rw["v1"], rw["v2"] = bf(rwkv_v1[j - 1]), bf(rwkv_v2[j - 1])
        w["rwkv"].append(rw)

    mem_rows = mem_prompt.reshape(b_p * MEM_TOKENS, D_MODEL)
    mem4 = lambda t: t.reshape(DEPTH, b_p, MEM_TOKENS, D_MODEL)
    mk, mk16 = map(mem4, _mem_kv(mem_rows, xa_w_k, "mem_k"))
    mv, mv16 = map(mem4, _mem_kv(mem_rows, xa_w_v, "mem_v"))
    z_s5 = jnp.zeros((n_s5, b_p, S5_GROUPS, S5_STATE), F32)
    z_rwkv = jnp.zeros((n_rwkv, b_p, RWKV_HEADS, RWKV_HEAD, RWKV_HEAD), F32)
    z_shift = jnp.zeros((n_rwkv, b_p, D_MODEL), F32)
    assert b_p == SCAN_SEQS
    y_p, re_p, im_p, s_p, sh_p = _trunk(x_prompt, b_p, t_p, mk16, mv16, z_s5, z_s5, z_rwkv, z_shift, w)

    y_s, re_s, im_s, s_s, sh_s = _trunk(_time_major(x_sample), b_s, t_s, cache_mem_k, cache_mem_v,
                                        state_s5_re, state_s5_im, state_rwkv, state_shift, w)

    kv_shape = (DEPTH, b_p, MEM_TOKENS, XA_HEADS, XA_HEAD_DIM)
    return (y_p, _batch_major(y_s, b_s, t_s), mk.reshape(kv_shape), mv.reshape(kv_shape),
            re_p, im_p, s_p, sh_p, re_s, im_s, s_s, sh_s)
```

```python
import functools

import jax
import jax.numpy as jnp
from jax import lax
from jax.experimental import pallas as pl
from jax.experimental.pallas import tpu as pltpu

F32 = jnp.float32
BF16 = jnp.bfloat16

D_MODEL = 1024
DEPTH = 4
S5_GROUP = 16
S5_GROUPS = 64
S5_STATE = 64
S5_WIDTH = S5_GROUPS * S5_STATE
S5_BLOCKS = 4
S5_BLOCK_IN = D_MODEL // S5_BLOCKS
S5_BLOCK_ST = S5_WIDTH // S5_BLOCKS
RWKV_HEADS = 16
RWKV_HEAD = 64
RWKV_PAIRS = RWKV_HEADS // 2
PAIR_W = 2 * RWKV_HEAD
RWKV_CHUNK = 64
RWKV_GN_EPS = 64e-5
SCAN_SEQS = 8
SCAN_LANES = 512
SCAN_PAIRS = SCAN_LANES // PAIR_W
SCAN_GROUP = 4
SHORT_GROUPS = 2
MEM_TOKENS = 256
XA_HEADS = 4
XA_HEAD_DIM = 256
D_FF = 4 * D_MODEL
FF_CHUNK = 1024
LN_EPS = 1e-5
ALPHA = (2.0 * DEPTH) ** 0.25

V7X_VMEM_LIMIT = 56 * 1024 * 1024
ROW_TILE = 512
XA_ROWS = 1024


def _dot(a, b):
    return jnp.dot(a, b, preferred_element_type=F32)


def _dot_nt(a, b):
    return lax.dot_general(a, b, (((1,), (1,)), ((), ())), preferred_element_type=F32)


def _layer_norm(z, g, b):
    mu = jnp.mean(z, axis=-1, keepdims=True)
    zc = z - mu
    var = jnp.mean(zc * zc, axis=-1, keepdims=True)
    return zc * lax.rsqrt(var + LN_EPS) * g + b


def _to_seq_major(x):
    return jnp.transpose(x.reshape(x.shape[0] // 8, 8, x.shape[1]), (1, 0, 2))


def _to_time_major(x3):
    return jnp.transpose(x3, (1, 0, 2)).reshape(x3.shape[0] * x3.shape[1], x3.shape[2])


def _load_rows(ref):
    return ref[...] if len(ref.shape) == 2 else _to_time_major(ref[...])


def _store_rows(ref, val):
    ref[...] = val if len(ref.shape) == 2 else _to_seq_major(val)


def _const_spec(shape):
    nd = len(shape)
    return pl.BlockSpec(tuple(shape), lambda *_: (0,) * nd, pipeline_mode=pl.Buffered(1))


def _layer_spec(stacked, layer):
    nd = stacked.ndim - 1
    return pl.BlockSpec((None,) + tuple(stacked.shape[1:]), lambda *_: (layer,) + (0,) * nd,
                        pipeline_mode=pl.Buffered(1))


def _params(*sem):
    return pltpu.CompilerParams(dimension_semantics=sem, vmem_limit_bytes=V7X_VMEM_LIMIT)


def _rows_call(body, name, rows, tm, row_ins, const_ins, outs, *, scratch=(), sequential=False,
               const_outs=(), seq_major_out=False):
    seq_spec = lambda c: pl.BlockSpec((SCAN_SEQS, tm // SCAN_SEQS, c), lambda i: (0, i, 0))
    in_specs = [seq_spec(a.shape[-1]) if a.ndim == 3 else pl.BlockSpec((tm, a.shape[-1]), lambda i: (i, 0))
                for a in row_ins]
    const_ins = [c if isinstance(c, tuple) else (c, None) for c in const_ins]
    in_specs += [_const_spec(a.shape) if layer is None else _layer_spec(a, layer) for a, layer in const_ins]
    if seq_major_out:
        out_specs = [seq_spec(c) for c, _ in outs]
        out_shape = [jax.ShapeDtypeStruct((SCAN_SEQS, rows // SCAN_SEQS, c), dt) for c, dt in outs]
    else:
        out_specs = [pl.BlockSpec((tm, c), lambda i: (i, 0)) for c, _ in outs]
        out_shape = [jax.ShapeDtypeStruct((rows, c), dt) for c, dt in outs]
    for shp, dt in const_outs:
        nd = len(shp)
        out_specs.append(pl.BlockSpec(tuple(shp), lambda i, nd=nd: (0,) * nd))
        out_shape.append(jax.ShapeDtypeStruct(tuple(shp), dt))
    return pl.pallas_call(
        body, grid=(rows // tm,), in_specs=in_specs, out_specs=out_specs, out_shape=out_shape,
        scratch_shapes=list(scratch),
        compiler_params=_params("arbitrary" if sequential else "parallel"), name=name,
    )(*row_ins, *[a for a, _ in const_ins])


def _matmul_body(x_ref, w_ref, o_ref):
    o_ref[...] = _dot(x_ref[...].astype(BF16), w_ref[...]).astype(o_ref.dtype)


def _matmul(x, w, name):
    rows = x.shape[0]
    return _rows_call(_matmul_body, name, rows, min(ROW_TILE, rows), [x], [w], [(w[0].shape[-1], F32)])[0]


def _memkv_body(x_ref, w_ref, o_ref, o16_ref):
    y = _dot(x_ref[...].astype(BF16), w_ref[0].astype(BF16))
    o_ref[0] = y
    o16_ref[0] = y.astype(BF16)


def _mem_kv(mem_rows, w_stack, name):
    rows = mem_rows.shape[0]
    out_spec = pl.BlockSpec((1, rows, D_MODEL), lambda l: (l, 0, 0))
    return pl.pallas_call(
        _memkv_body, grid=(DEPTH,),
        in_specs=[_const_spec(mem_rows.shape), pl.BlockSpec((1, D_MODEL, D_MODEL), lambda l: (l, 0, 0))],
        out_specs=[out_spec, out_spec],
        out_shape=[jax.ShapeDtypeStruct((DEPTH, rows, D_MODEL), F32),
                   jax.ShapeDtypeStruct((DEPTH, rows, D_MODEL), BF16)],
        compiler_params=_params("parallel"), name=name,
    )(mem_rows, w_stack)


def _proj_ln_body(x_ref, o_ref, w_ref, g_ref, b_ref, out_ref):
    h = _dot(o_ref[...].astype(BF16), w_ref[...])
    out_ref[...] = _layer_norm(ALPHA * x_ref[...] + h, g_ref[...], b_ref[...])


def _mlp_body(x_ref, w1_ref, w2_ref, g_ref, b_ref, out_ref):
    x = x_ref[...]
    xb = x.astype(BF16)
    acc = None
    for c in range(D_FF // FF_CHUNK):
        h = _dot(xb, w1_ref[:, c * FF_CHUNK:(c + 1) * FF_CHUNK])
        h = jnp.maximum(h, 0.0)
        p = _dot((h * h).astype(BF16), w2_ref[c * FF_CHUNK:(c + 1) * FF_CHUNK, :])
        acc = p if acc is None else acc + p
    _store_rows(out_ref, _layer_norm(ALPHA * x + acc, g_ref[...], b_ref[...]))


def _softmax_rows(s):
    m = jnp.max(s, axis=-1, keepdims=True)
    e = jnp.exp(s - m)
    return e / jnp.sum(e, axis=-1, keepdims=True)


def _xattn_fused_body(x_ref, wq_ref, k_ref, v_ref, wo_ref, g_ref, b_ref, out_ref):
    x = x_ref[...]
    q3 = _to_seq_major(_dot(x.astype(BF16), wq_ref[...])).astype(BF16)
    heads = [slice(h * XA_HEAD_DIM, (h + 1) * XA_HEAD_DIM) for h in range(XA_HEADS)]
    units = [(b, sl) for b in range(SCAN_SEQS) for sl in heads]
    s = [_dot_nt(q3[b][:, sl], k_ref[b][:, sl]) * (XA_HEAD_DIM ** -0.5) for b, sl in units]
    p = [_softmax_rows(t).astype(BF16) for t in s]
    pv = [_dot(p[u], v_ref[b][:, sl]) for u, (b, sl) in enumerate(units)]
    per_seq = [jnp.concatenate(pv[b * XA_HEADS:(b + 1) * XA_HEADS], axis=-1) for b in range(SCAN_SEQS)]
    o = _to_time_major(jnp.stack(per_seq, axis=0))
    out_ref[...] = _layer_norm(ALPHA * x + _dot(o.astype(BF16), wo_ref[...]), g_ref[...], b_ref[...])


def _xattn_fused(x, wq, mem_k, mem_v, wo, ln_g, ln_b, layer):
    rows = x.shape[0]
    tm = min(XA_ROWS, rows)
    tile = pl.BlockSpec((tm, D_MODEL), lambda i: (i, 0))
    kv_spec = pl.BlockSpec((None, SCAN_SEQS, MEM_TOKENS, D_MODEL), lambda i: (layer, 0, 0, 0),
                           pipeline_mode=pl.Buffered(1))
    vec = _const_spec((1, D_MODEL))
    return pl.pallas_call(
        _xattn_fused_body, grid=(rows // tm,),
        in_specs=[tile, _layer_spec(*wq), kv_spec, kv_spec, _layer_spec(*wo), vec, vec],
        out_specs=tile, out_shape=jax.ShapeDtypeStruct((rows, D_MODEL), F32),
        compiler_params=_params("parallel"), name="xattn_fused",
    )(x, wq[0], mem_k, mem_v, wo[0], ln_g, ln_b)


def _attn_cache_body(q_ref, k_ref, v_ref, o_ref):
    t_len, n_seq = q_ref.shape[0], q_ref.shape[1]
    shape = (XA_HEADS * t_len, MEM_TOKENS * XA_HEADS)
    own_head = (lax.broadcasted_iota(jnp.int32, shape, 0) // t_len
                == lax.broadcasted_iota(jnp.int32, shape, 1) % XA_HEADS)
    q3 = _to_seq_major(q_ref[...].reshape(t_len * n_seq, D_MODEL))
    outs = []
    for b in range(n_seq):
        q = q3[b]
        qs = jnp.concatenate([q[:, h * XA_HEAD_DIM:(h + 1) * XA_HEAD_DIM] for h in range(XA_HEADS)], axis=0)
        k2 = k_ref[b].reshape(MEM_TOKENS * XA_HEADS, XA_HEAD_DIM).astype(BF16)
        v2 = v_ref[b].reshape(MEM_TOKENS * XA_HEADS, XA_HEAD_DIM).astype(BF16)
        s = _dot_nt(qs.astype(BF16), k2) * (XA_HEAD_DIM ** -0.5)
        p = _softmax_rows(jnp.where(own_head, s, -1e30))
        o = _dot(p.astype(BF16), v2)
        outs.append(jnp.concatenate([o[h * t_len:(h + 1) * t_len] for h in range(XA_HEADS)], axis=-1))
    o_ref[...] = _to_time_major(jnp.stack(outs, axis=0)).reshape(t_len, n_seq, D_MODEL)


def _attention_cache(q_rows, cache_k, cache_v, layer, batch, t_len):
    nb = SCAN_SEQS
    assert t_len == 8 and batch % nb == 0
    kv_spec = pl.BlockSpec((None, nb, MEM_TOKENS, XA_HEADS, XA_HEAD_DIM), lambda b: (layer, b, 0, 0, 0))
    tile = pl.BlockSpec((t_len, nb, D_MODEL), lambda b: (0, b, 0))
    return pl.pallas_call(
        _attn_cache_body, grid=(batch // nb,), in_specs=[tile, kv_spec, kv_spec], out_specs=tile,
        out_shape=jax.ShapeDtypeStruct((t_len, batch, D_MODEL), F32),
        compiler_params=_params("parallel"), name="xattn_cache",
    )(q_rows.reshape(t_len, batch, D_MODEL), cache_k, cache_v).reshape(t_len * batch, D_MODEL)


def _s5_prep_body(lre_ref, lim_ref, ldt_ref, br_ref, bi_ref, are_ref, aim_ref, bbre_ref, bbim_ref):
    lre = lre_ref[...]
    lim = lim_ref[...]
    dt = jnp.exp(ldt_ref[...])
    mag = jnp.exp(lre * dt)
    ang = lim * dt
    ab_re = mag * jnp.cos(ang)
    ab_im = mag * jnp.sin(ang)
    den = lre * lre + lim * lim
    num_re = ab_re - 1.0
    f_re = (num_re * lre + ab_im * lim) / den
    f_im = (ab_im * lre - num_re * lim) / den
    br = br_ref[...]
    bi = bi_ref[...]
    are_ref[...] = ab_re
    aim_ref[...] = ab_im
    bbre_ref[...] = f_re * br - f_im * bi
    bbim_ref[...] = f_re * bi + f_im * br


def _s5_prep(a_re, a_im, log_dt, b_re, b_im, c_re, c_im):
    rep = lambda t: jnp.repeat(t, S5_GROUP, axis=0)
    to_rows = lambda t: jnp.transpose(t, (0, 2, 1)).reshape(S5_GROUPS * S5_GROUP, S5_STATE)
    ins = [rep(a_re), rep(a_im), rep(jnp.broadcast_to(log_dt[:, None], (S5_GROUPS, S5_STATE))),
           to_rows(b_re), to_rows(b_im)]
    shp = jax.ShapeDtypeStruct((S5_GROUPS * S5_GROUP, S5_STATE), F32)
    full = pl.BlockSpec((S5_GROUPS * S5_GROUP, S5_STATE), lambda: (0, 0))
    are, aim, bbre, bbim = pl.pallas_call(
        _s5_prep_body, in_specs=[full] * 5, out_specs=[full] * 4, out_shape=[shp] * 4, name="s5_prep",
    )(*ins)
    a_row = lambda t: t[::S5_GROUP].reshape(1, S5_WIDTH)
    eye = jnp.eye(S5_GROUP, dtype=F32)
    blk_b = lambda t: jnp.einsum('qgcn,gh->qgchn', t.reshape(S5_BLOCKS, 16, S5_GROUP, S5_STATE), eye
                                 ).reshape(S5_BLOCKS, S5_BLOCK_IN, S5_BLOCK_ST)
    wb = jnp.concatenate([blk_b(bbre), blk_b(bbim)], axis=2).astype(BF16)
    blk_c = lambda t: jnp.einsum('qgcn,gh->qhngc', t.reshape(S5_BLOCKS, 16, S5_GROUP, S5_STATE), eye
                                 ).reshape(S5_BLOCKS, S5_BLOCK_ST, S5_BLOCK_IN)
    wc = jnp.concatenate([blk_c(c_re), -blk_c(c_im)], axis=1).astype(BF16)
    return a_row(are), a_row(aim), wb, wc


def _s5_body(batch, x_ref, h0re_ref, h0im_ref, are_ref, aim_ref, wb_ref, wc_ref, d_ref, wv_ref, wg_ref,
             g_ref, b_ref, out_ref, hre_ref, him_ref, hbuf, sre, sim):
    x = _load_rows(x_ref)
    rows = x.shape[0]
    steps = rows // batch

    @pl.when(pl.program_id(0) == 0)
    def _():
        sre[...] = h0re_ref[...]
        sim[...] = h0im_ref[...]

    xb = x.astype(BF16)
    for q in range(S5_BLOCKS):
        hbuf[:, q * 2 * S5_BLOCK_ST:(q + 1) * 2 * S5_BLOCK_ST] = _dot(
            xb[:, q * S5_BLOCK_IN:(q + 1) * S5_BLOCK_IN], wb_ref[q])

    def row_group(rg, carry):
        r_off = pl.multiple_of(rg * 8, 8)
        for q in range(S5_BLOCKS):
            st = slice(q * S5_BLOCK_ST, (q + 1) * S5_BLOCK_ST)
            c_re = slice(q * 2 * S5_BLOCK_ST, q * 2 * S5_BLOCK_ST + S5_BLOCK_ST)
            c_im = slice(q * 2 * S5_BLOCK_ST + S5_BLOCK_ST, (q + 1) * 2 * S5_BLOCK_ST)
            ar = jnp.broadcast_to(are_ref[:, st], (8, S5_BLOCK_ST))
            ai = jnp.broadcast_to(aim_ref[:, st], (8, S5_BLOCK_ST))

            def step(t, h):
                hr, hi = h
                row = pl.multiple_of(t * batch + r_off, 8)
                nr = ar * hr - ai * hi + hbuf[pl.ds(row, 8), c_re]
                ni = ar * hi + ai * hr + hbuf[pl.ds(row, 8), c_im]
                hbuf[pl.ds(row, 8), c_re] = nr
                hbuf[pl.ds(row, 8), c_im] = ni
                return nr, ni

            hr, hi = lax.fori_loop(0, steps, step, (sre[pl.ds(r_off, 8), st], sim[pl.ds(r_off, 8), st]),
                                   unroll=4)
            sre[pl.ds(r_off, 8), st] = hr
            sim[pl.ds(r_off, 8), st] = hi
        return carry

    lax.fori_loop(0, batch // 8, row_group, 0)
    hre_ref[...] = sre[...]
    him_ref[...] = sim[...]

    ys = [_dot(hbuf[:, q * 2 * S5_BLOCK_ST:(q + 1) * 2 * S5_BLOCK_ST].astype(BF16), wc_ref[q])
          for q in range(S5_BLOCKS)]
    y = jnp.concatenate(ys, axis=-1) + d_ref[...] * x
    z = jax.nn.gelu(y).astype(BF16)
    h = _dot(z, wv_ref[...]) * jax.nn.sigmoid(_dot(z, wg_ref[...]))
    out_ref[...] = _layer_norm(ALPHA * x + h, g_ref[...], b_ref[...])


def _s5_layer(x, batch, h0re, h0im, s5w, ln_g, ln_b):
    rows = x.size // D_MODEL
    tm = min(ROW_TILE, rows)
    a_re, a_im, wb, wc, d_vec, wv, wg = s5w
    out, hre, him = _rows_call(
        functools.partial(_s5_body, batch), "s5_mixer", rows, tm, [x],
        [h0re, h0im, a_re, a_im, wb, wc, d_vec, wv, wg, ln_g, ln_b], [(D_MODEL, F32)],
        scratch=[pltpu.VMEM((tm, 2 * S5_WIDTH), F32), pltpu.VMEM((batch, S5_WIDTH), F32),
                 pltpu.VMEM((batch, S5_WIDTH), F32)],
        sequential=True, const_outs=[((batch, S5_WIDTH), F32)] * 2)
    return out, hre, him


def _head_sum(v, ones_ref):
    return _dot(v.astype(BF16), ones_ref[...])


def _token_shift_mix(batch, x_ref, shift_ref, mu_ref, xprev):
    rows = x_ref.shape[0]

    @pl.when(pl.program_id(0) == 0)
    def _():
        xprev[...] = shift_ref[...]

    x = x_ref[...]
    if rows > batch:
        xs = jnp.concatenate([xprev[...], x[:rows - batch]], axis=0)
    else:
        xs = xprev[...]
    xprev[...] = x[rows - batch:]
    xx = xs - x
    return lambda i: (x + xx * mu_ref[i:i + 1, :]).astype(BF16)


def _rwkv_proj_rk_body(batch, x_ref, shift_ref, mu_ref, wr_ref, wk_ref, a1_ref, a2_ref, vec_ref, ones_ref,
                       r_out, k_out, a_out, b_out, xprev):
    mix = _token_shift_mix(batch, x_ref, shift_ref, mu_ref, xprev)
    a0, k_k, k_a = vec_ref[1:2, :], vec_ref[3:4, :], vec_ref[4:5, :]
    _store_rows(r_out, _dot(mix(0), wr_ref[...]))
    k = _dot(mix(2), wk_ref[...])
    a = jax.nn.sigmoid(a0 + _dot(_dot(mix(4), a1_ref[...]).astype(BF16), a2_ref[...]))
    kk = k * k_k
    sq = kk * kk
    sq_hi = sq.astype(BF16)
    ss = _dot(sq_hi, ones_ref[...]) + _head_sum(sq - sq_hi.astype(F32), ones_ref)
    kk = kk / jnp.maximum(jnp.sqrt(ss), 1e-12)
    _store_rows(k_out, k * (1.0 + (a - 1.0) * k_a))
    _store_rows(a_out, -kk)
    _store_rows(b_out, kk * a)


def _rwkv_proj_vwg_body(batch, has_vgate, *refs):
    if has_vgate:
        (x_ref, vf_ref, shift_ref, mu_ref, wv_ref, w1_ref, w2_ref, g1_ref, g2_ref, v1_ref, v2_ref, vec_ref,
         lw_out, v_out, g_out, xprev) = refs
    else:
        (x_ref, shift_ref, mu_ref, wv_ref, w1_ref, w2_ref, g1_ref, g2_ref, vec_ref,
         lw_out, v_out, g_out, xprev) = refs
    mix = _token_shift_mix(batch, x_ref, shift_ref, mu_ref, xprev)
    w0, v0 = vec_ref[0:1, :], vec_ref[2:3, :]
    xv = mix(3)
    v = _dot(xv, wv_ref[...])
    w_lora = _dot(jnp.tanh(_dot(mix(1), w1_ref[...])).astype(BF16), w2_ref[...])
    w_log = -jax.nn.softplus(-(w0 + w_lora)) - 0.5
    _store_rows(lw_out, -jnp.exp(w_log))
    if has_vgate:
        gate_v = jax.nn.sigmoid(v0 + _dot(_dot(xv, v1_ref[...]).astype(BF16), v2_ref[...]))
        v = v + (_load_rows(vf_ref) - v) * gate_v
    _store_rows(v_out, v)
    _store_rows(g_out, _dot(jax.nn.sigmoid(_dot(mix(5), g1_ref[...])).astype(BF16), g2_ref[...]))


def _rwkv_consts(seq_len=RWKV_CHUNK):
    L = RWKV_CHUNK
    lane = lax.broadcasted_iota(jnp.int32, (1, PAIR_W), 1)
    t_idx = lax.broadcasted_iota(jnp.int32, (L, PAIR_W), 0)
    s_idx = lax.broadcasted_iota(jnp.int32, (L, PAIR_W), 1) % L
    r2 = lax.broadcasted_iota(jnp.int32, (2 * L, PAIR_W), 0)
    c2 = lax.broadcasted_iota(jnp.int32, (2 * L, PAIR_W), 1)
    rr = lax.broadcasted_iota(jnp.int32, (L, L), 0)
    cc = lax.broadcasted_iota(jnp.int32, (L, L), 1)
    strict_lower, incl_lower, tri = s_idx < t_idx, s_idx <= t_idx, cc <= rr
    cst = {}
    if seq_len < L:
        same_seq = (s_idx // seq_len) == (t_idx // seq_len)
        strict_lower = same_seq & strict_lower
        incl_lower = same_seq & incl_lower
        tri = ((cc // seq_len) == (rr // seq_len)) & tri
        cst["seq_ones"] = jnp.where((cc // seq_len) == (rr // seq_len), 1.0, 0.0).astype(BF16)
        cst["col_seq"] = (c2 % L) // seq_len
    cst.update(
        m0=lane < RWKV_HEAD,
        strict_lower=strict_lower, incl_lower=incl_lower,
        same_head=(r2 < L) == (c2 < RWKV_HEAD),
        eye2=jnp.where(r2 == c2, 1.0, 0.0).astype(F32),
        tri=jnp.where(tri, 1.0, 0.0).astype(BF16),
        zeros_l=jnp.zeros((L, PAIR_W), F32))
    return cst


def _expand_state(sc, m0):
    return jnp.concatenate([jnp.where(m0, sc, 0.0), jnp.where(m0, 0.0, sc)], axis=0)


def _compact_state(s_bd):
    return s_bd[0:RWKV_HEAD] + s_bd[RWKV_HEAD:PAIR_W]


def _rwkv_chunk(slabs, states, cst, seq_len=RWKV_CHUNK):
    L = RWKV_CHUNK
    n_seq = L // seq_len
    m0 = cst["m0"]
    bf = lambda t: t.astype(BF16)
    cat = lambda ts: jnp.concatenate(ts, axis=0)
    ap, rp, vp, btp, ktp, wbp, wkp, dtot = [], [], [], [], [], [], [], []
    for r, lw, k, v, a, b in slabs:
        h1 = bf(lw)
        e1 = lw - h1.astype(F32)
        h2 = bf(e1)
        h3 = bf(e1 - h2.astype(F32))
        cum = _dot(cst["tri"], h1) + _dot(cst["tri"], h2) + _dot(cst["tri"], h3)
        if n_seq == 1:
            tot = cum[L - 1:L, :]
        else:
            tot = _dot(cst["seq_ones"], h1) + _dot(cst["seq_ones"], h2) + _dot(cst["seq_ones"], h3)
        inv = jnp.exp(-cum)
        tail = jnp.exp(tot - cum)
        pieces = (a * jnp.exp(cum - lw), r * jnp.exp(cum), v, b * inv, k * inv, b * tail, k * tail,
                  jnp.exp(tot))
        for p in range(r.shape[1] // PAIR_W):
            sl = slice(p * PAIR_W, (p + 1) * PAIR_W)
            for dst, src in zip((ap, rp, vp, btp, ktp, wbp, wkp, dtot), pieces):
                dst.append(src[:, sl])
    P = range(len(ap))
    gram = [_dot_nt(bf(cat([ap[p], rp[p]])),
                    bf(cat([jnp.where(m0, btp[p], 0.0), jnp.where(m0, ktp[p], 0.0),
                            jnp.where(m0, 0.0, btp[p]), jnp.where(m0, 0.0, ktp[p])]))) for p in P]
    if n_seq == 1:
        ar_s = [_dot_nt(bf(cat([ap[p], rp[p]])), bf(states[p])) for p in P]
    else:
        ar_s = []
        for p in P:
            parts = [_dot_nt(bf(cat([ap[p][s * seq_len:(s + 1) * seq_len], rp[p][s * seq_len:(s + 1) * seq_len]])),
                             bf(states[p * n_seq + s])) for s in range(n_seq)]
            ar_s.append(cat([o[0:seq_len] for o in parts] + [o[seq_len:2 * seq_len] for o in parts]))
    na0 = [jnp.where(cst["strict_lower"], gram[p][0:L, 0:PAIR_W], 0.0) for p in P]
    na1 = [jnp.where(cst["strict_lower"], gram[p][0:L, PAIR_W:2 * PAIR_W], 0.0) for p in P]
    nr = [cat([jnp.where(cst["incl_lower"], gram[p][L:2 * L, 0:PAIR_W], 0.0),
               jnp.where(cst["incl_lower"], gram[p][L:2 * L, PAIR_W:2 * PAIR_W], 0.0)]) for p in P]
    nv = [_dot(bf(cat([na0[p], na1[p]])), bf(cat([cst["zeros_l"], vp[p]]))) for p in P]
    pw = [cat([jnp.where(m0, na0[p], 0.0), jnp.where(m0, 0.0, pltpu.roll(na1[p], RWKV_HEAD, axis=1))])
          for p in P]
    t_inv = [cst["eye2"] + pw[p] for p in P]
    for _ in range(seq_len.bit_length() - 2):
        pwb = [bf(pw[p]) for p in P]
        pw = [_dot(pwb[p], pwb[p]) for p in P]
        t_inv = [t_inv[p] + _dot(bf(t_inv[p]), bf(pw[p])) for p in P]
    u_st = [_dot(bf(t_inv[p]), bf(cat([ar_s[p][0:L], ar_s[p][0:L]]) + nv[p])) for p in P]
    zz = [cat([jnp.where(m0, u_st[p][0:L], u_st[p][L:2 * L]), vp[p]]) for p in P]
    y_st = [_dot(bf(nr[p]), bf(zz[p])) for p in P]
    ys = [ar_s[p][L:2 * L] + jnp.where(m0, y_st[p][0:L], y_st[p][L:2 * L]) for p in P]
    if n_seq == 1:
        upd = [_dot(bf(zz[p].T), bf(cat([wbp[p], wkp[p]]))) for p in P]
        new_states = [states[p] * dtot[p] + jnp.where(cst["same_head"], upd[p], 0.0) for p in P]
    else:
        new_states = []
        for p in P:
            zz_t = zz[p].T
            lhs = cat([jnp.where(cst["col_seq"] == s, zz_t, 0.0) for s in range(n_seq)])
            upd = _dot(bf(lhs), bf(cat([wbp[p], wkp[p]])))
            for s in range(n_seq):
                new_states.append(states[p * n_seq + s] * dtot[p][s * seq_len:s * seq_len + 1]
                                  + jnp.where(cst["same_head"], upd[s * PAIR_W:(s + 1) * PAIR_W], 0.0))
    y_slabs, at = [], 0
    for slab in slabs:
        n_i = slab[0].shape[1] // PAIR_W
        y_slabs.append(jnp.concatenate(ys[at:at + n_i], axis=1) if n_i > 1 else ys[at])
        at += n_i
    return y_slabs, new_states


def _rwkv_scan_short_body(r_ref, lw_ref, k_ref, v_ref, a_ref, b_ref, s0_ref, y_ref, sout_ref):
    t_len, n_seq = r_ref.shape[0], SCAN_SEQS
    cst = _rwkv_consts(t_len)
    slabs, states = [], []
    for g in range(SHORT_GROUPS):
        rows = lambda ref: _to_seq_major(ref[:, g * n_seq:(g + 1) * n_seq, :].reshape(t_len * n_seq, D_MODEL)
                                         ).reshape(n_seq * t_len, D_MODEL)
        slabs.append(tuple(rows(ref) for ref in (r_ref, lw_ref, k_ref, v_ref, a_ref, b_ref)))
        states += [_expand_state(s0_ref[g * n_seq + s, p], cst["m0"])
                   for p in range(RWKV_PAIRS) for s in range(n_seq)]
    ys, new_states = _rwkv_chunk(slabs, states, cst, t_len)
    for g in range(SHORT_GROUPS):
        y_ref[:, g * n_seq:(g + 1) * n_seq, :] = _to_time_major(
            ys[g].reshape(n_seq, t_len, D_MODEL)).reshape(t_len, n_seq, D_MODEL)
        for s in range(n_seq):
            for p in range(RWKV_PAIRS):
                sout_ref[g * n_seq + s, p] = _compact_state(new_states[(g * RWKV_PAIRS + p) * n_seq + s])


def _rwkv_scan_seq_body(r_ref, lw_ref, k_ref, v_ref, a_ref, b_ref, s0_ref, y_ref, sout_ref, s_scr):
    cst = _rwkv_consts()
    in_refs = (r_ref, lw_ref, k_ref, v_ref, a_ref, b_ref)

    @pl.when(pl.program_id(1) == 0)
    def _():
        for b in range(SCAN_SEQS):
            for p in range(SCAN_PAIRS):
                s_scr[b * SCAN_PAIRS + p] = _expand_state(s0_ref[b, p], cst["m0"])

    def sequence_group(i, carry):
        seqs = tuple(SCAN_GROUP * i + n for n in range(SCAN_GROUP))
        slabs = [tuple(ref[s] for ref in in_refs) for s in seqs]
        states = [s_scr[s * SCAN_PAIRS + p] for s in seqs for p in range(SCAN_PAIRS)]
        y_slabs, new_states = _rwkv_chunk(slabs, states, cst)
        for n, s in enumerate(seqs):
            y_ref[s] = y_slabs[n]
            for p in range(SCAN_PAIRS):
                s_scr[s * SCAN_PAIRS + p] = new_states[n * SCAN_PAIRS + p]
        return carry

    lax.fori_loop(0, SCAN_SEQS // SCAN_GROUP, sequence_group, 0)

    @pl.when(pl.program_id(1) == pl.num_programs(1) - 1)
    def _():
        for b in range(SCAN_SEQS):
            for p in range(SCAN_PAIRS):
                sout_ref[b, p] = _compact_state(s_scr[b * SCAN_PAIRS + p])


def _rwkv_out_body(y_ref, r_ref, k_ref, v_ref, g_ref, x_ref, ones_ref, vec_ref, wo_ref, lg_ref, lb_ref, out_ref):
    per_seq = len(y_ref.shape) == 3
    flat = lambda ref: ref[...].reshape(-1, D_MODEL) if per_seq else ref[...]
    x = x_ref[...]
    if per_seq:
        x = _to_seq_major(x).reshape(x.shape)
    y = flat(y_ref)
    r_k, lnx_g, lnx_b = (vec_ref[i:i + 1, :] for i in range(3))
    inv_n = 1.0 / RWKV_HEAD
    mu = _head_sum(y, ones_ref) * inv_n
    yc = y - mu
    var = _head_sum(yc * yc, ones_ref) * inv_n
    yn = yc * lax.rsqrt(var + RWKV_GN_EPS) * lnx_g + lnx_b
    bonus = _head_sum(flat(r_ref) * flat(k_ref) * r_k, ones_ref) * flat(v_ref)
    o = ((yn + bonus) * flat(g_ref)).astype(BF16)
    res = _layer_norm(ALPHA * x + _dot(o, wo_ref[...]), lg_ref[...], lb_ref[...])
    out_ref[...] = _to_time_major(res.reshape(y_ref.shape)) if per_seq else res


def _pair_state(s):
    bsz = s.shape[0]
    s = s.reshape(bsz, RWKV_PAIRS, 2, RWKV_HEAD, RWKV_HEAD)
    return jnp.transpose(s, (0, 1, 3, 2, 4)).reshape(bsz, RWKV_PAIRS, RWKV_HEAD, PAIR_W)


def _unpair_state(s):
    bsz = s.shape[0]
    s = s.reshape(bsz, RWKV_PAIRS, RWKV_HEAD, 2, RWKV_HEAD)
    return jnp.transpose(s, (0, 1, 3, 2, 4)).reshape(bsz, RWKV_HEADS, RWKV_HEAD, RWKV_HEAD)


def _rwkv_layer(x, batch, t_len, s0, shift0, v_first, rw, ln_g, ln_b):
    rows = x.shape[0]
    tm = min(ROW_TILE, rows)
    has_vgate = v_first is not None
    per_seq = batch == SCAN_SEQS
    shift_scratch = [pltpu.VMEM((batch, D_MODEL), F32)]
    r, k, a, b = _rows_call(
        functools.partial(_rwkv_proj_rk_body, batch), "rwkv_proj_rk", rows, tm, [x],
        [shift0, rw["mu"], rw["w_r"], rw["w_k"], rw["a1"], rw["a2"], rw["proj_vec"], rw["ones"]],
        [(D_MODEL, F32)] * 4, scratch=shift_scratch, sequential=True, seq_major_out=per_seq)
    consts = [shift0, rw["mu"], rw["w_v"], rw["w1"], rw["w2"], rw["g1"], rw["g2"]]
    if has_vgate:
        consts += [rw["v1"], rw["v2"]]
    lw, v, g = _rows_call(
        functools.partial(_rwkv_proj_vwg_body, batch, has_vgate), "rwkv_proj_vwg", rows, tm,
        [x] + ([v_first] if has_vgate else []), consts + [rw["proj_vec"]],
        [(D_MODEL, F32)] * 3, scratch=shift_scratch, sequential=True, seq_major_out=per_seq)

    if per_seq:
        assert t_len % RWKV_CHUNK == 0
        n_half = D_MODEL // SCAN_LANES
        tile = pl.BlockSpec((SCAN_SEQS, RWKV_CHUNK, SCAN_LANES), lambda h, c: (0, c, h))
        st_spec = pl.BlockSpec((batch, None, SCAN_PAIRS, RWKV_HEAD, PAIR_W), lambda h, c: (0, h, 0, 0, 0))
        st_shape = (batch, n_half, SCAN_PAIRS, RWKV_HEAD, PAIR_W)
        y, s_last = pl.pallas_call(
            _rwkv_scan_seq_body, grid=(n_half, t_len // RWKV_CHUNK),
            in_specs=[tile] * 6 + [st_spec], out_specs=[tile, st_spec],
            out_shape=[jax.ShapeDtypeStruct((batch, t_len, D_MODEL), F32), jax.ShapeDtypeStruct(st_shape, F32)],
            scratch_shapes=[pltpu.VMEM((SCAN_SEQS * SCAN_PAIRS, PAIR_W, PAIR_W), F32)],
            compiler_params=_params("arbitrary", "arbitrary"), name="rwkv_scan",
        )(r, lw, k, v, a, b, _pair_state(s0).reshape(st_shape))
        s_last = s_last.reshape(batch, RWKV_PAIRS, RWKV_HEAD, PAIR_W)
    else:
        n_seq = SCAN_SEQS * SHORT_GROUPS
        assert t_len * SCAN_SEQS == RWKV_CHUNK and batch % n_seq == 0
        view = lambda t: t.reshape(t_len, batch, D_MODEL)
        tile = pl.BlockSpec((t_len, n_seq, D_MODEL), lambda bi: (0, bi, 0))
        st_spec = pl.BlockSpec((n_seq, RWKV_PAIRS, RWKV_HEAD, PAIR_W), lambda bi: (bi, 0, 0, 0))
        y2, s_last = pl.pallas_call(
            _rwkv_scan_short_body, grid=(batch // n_seq,),
            in_specs=[tile] * 6 + [st_spec], out_specs=[tile, st_spec],
            out_shape=[jax.ShapeDtypeStruct((t_len, batch, D_MODEL), F32),
                       jax.ShapeDtypeStruct((batch, RWKV_PAIRS, RWKV_HEAD, PAIR_W), F32)],
            compiler_params=_params("parallel"), name="rwkv_scan_short",
        )(view(r), view(lw), view(k), view(v), view(a), view(b), _pair_state(s0))
        y = y2.reshape(rows, D_MODEL)

    out = _rows_call(_rwkv_out_body, "rwkv_out", rows, tm, [y, r, k, v, g, x],
                     [rw["ones"], rw["out_vec"], rw["w_o"], ln_g, ln_b], [(D_MODEL, F32)])[0]
    return out, _unpair_state(s_last), x[rows - batch:], (v if not has_vgate else v_first)


def _trunk(x, batch, t_len, mem_k, mem_v, s5_re, s5_im, rwkv_s, shift, w):
    rows = x.size // D_MODEL
    tm = min(ROW_TILE, rows)
    new_re, new_im, new_s, new_shift = [], [], [], []
    v_first = None
    for i in range(DEPTH):
        j = i // 2
        ln = lambda n: (w["ln_g"][i, n][None, :], w["ln_b"][i, n][None, :])
        if i % 2 == 0:
            x, hre, him = _s5_layer(x, batch, s5_re[j].reshape(batch, S5_WIDTH),
                                    s5_im[j].reshape(batch, S5_WIDTH), w["s5"][j], *ln(0))
            new_re.append(hre.reshape(batch, S5_GROUPS, S5_STATE))
            new_im.append(him.reshape(batch, S5_GROUPS, S5_STATE))
        else:
            x, s_last, x_last, v_first = _rwkv_layer(x, batch, t_len, rwkv_s[j], shift[j], v_first,
                                                     w["rwkv"][j], *ln(0))
            new_s.append(s_last)
            new_shift.append(x_last)
        if batch == SCAN_SEQS:
            x = _xattn_fused(x, w["xa_q"][i], mem_k, mem_v, w["xa_o"][i], *ln(1), i)
        else:
            o = _attention_cache(_matmul(x, w["xa_q"][i], "xattn_q"), mem_k, mem_v, i, batch, t_len)
            x = _rows_call(_proj_ln_body, "xattn_out", rows, tm, [x, o],
                           [w["xa_o"][i], *ln(1)], [(D_MODEL, F32)])[0]
        x = _rows_call(_mlp_body, "mlp", rows, tm, [x], [w["mlp1"][i], w["mlp2"][i], *ln(2)], [(D_MODEL, F32)],
                       seq_major_out=(batch == SCAN_SEQS and i == DEPTH - 1))[0]
    return x, jnp.stack(new_re), jnp.stack(new_im), jnp.stack(new_s), jnp.stack(new_shift)


def _time_major(x):
    bsz, t_len, d = x.shape
    return jnp.transpose(x, (1, 0, 2)).reshape(t_len * bsz, d)


def _batch_major(x, bsz, t_len):
    return jnp.transpose(x.reshape(t_len, bsz, -1), (1, 0, 2))


def kernel(x_prompt, x_sample, mem_prompt, cache_mem_k, cache_mem_v, state_s5_re, state_s5_im, state_rwkv, state_shift, ln_g, ln_b, s5_a_re, s5_a_im, s5_log_dt, s5_b_re, s5_b_im, s5_c_re, s5_c_im, s5_d, s5_w_glu_v, s5_w_glu_g, rwkv_mu, rwkv_w_r, rwkv_w_k, rwkv_w_v, rwkv_w_o, rwkv_w0, rwkv_w1, rwkv_w2, rwkv_a0, rwkv_a1, rwkv_a2, rwkv_v0, rwkv_v1, rwkv_v2, rwkv_g1, rwkv_g2, rwkv_k_k, rwkv_k_a, rwkv_r_k, rwkv_lnx_g, rwkv_lnx_b, xa_w_q, xa_w_k, xa_w_v, xa_w_o, mlp_w1, mlp_w2):
    bf = lambda t: t.astype(BF16)
    n_s5, n_rwkv = state_s5_re.shape[0], state_rwkv.shape[0]
    b_p, t_p = x_prompt.shape[0], x_prompt.shape[1]
    b_s, t_s = x_sample.shape[0], x_sample.shape[1]

    head_id = jnp.arange(D_MODEL) // RWKV_HEAD
    ones_bd = (head_id[:, None] == head_id[None, :]).astype(BF16)
    per_layer = lambda stack, n: [(s16, i) for s16 in [bf(stack)] for i in range(n)]
    w = {"ln_g": ln_g, "ln_b": ln_b, "s5": [], "rwkv": [],
         "xa_q": per_layer(xa_w_q, DEPTH), "xa_o": per_layer(xa_w_o, DEPTH),
         "mlp1": per_layer(mlp_w1, DEPTH), "mlp2": per_layer(mlp_w2, DEPTH)}
    glu_v, glu_g = per_layer(s5_w_glu_v, n_s5), per_layer(s5_w_glu_g, n_s5)
    for j in range(n_s5):
        a_re, a_im, wb, wc = _s5_prep(s5_a_re[j], s5_a_im[j], s5_log_dt[j], s5_b_re[j], s5_b_im[j],
                                      s5_c_re[j], s5_c_im[j])
        w["s5"].append((a_re, a_im, wb, wc, s5_d[j][None, :], glu_v[j], glu_g[j]))
    stacks = {name: per_layer(t, n_rwkv) for name, t in (
        ("w_r", rwkv_w_r), ("w_k", rwkv_w_k), ("w_v", rwkv_w_v), ("w_o", rwkv_w_o), ("w1", rwkv_w1),
        ("w2", rwkv_w2), ("a1", rwkv_a1), ("a2", rwkv_a2), ("g1", rwkv_g1), ("g2", rwkv_g2))}
    v1, v2 = per_layer(rwkv_v1, n_rwkv - 1), per_layer(rwkv_v2, n_rwkv - 1)
    for j in range(n_rwkv):
        v0 = rwkv_v0[j - 1] if j > 0 else jnp.zeros((D_MODEL,), F32)
        rw = {name: stack[j] for name, stack in stacks.items()}
        rw.update(mu=rwkv_mu[j], ones=ones_bd,
                  proj_vec=jnp.stack([rwkv_w0[j], rwkv_a0[j], v0, rwkv_k_k[j], rwkv_k_a[j]]),
                  out_vec=jnp.stack([rwkv_r_k[j].reshape(D_MODEL), rwkv_lnx_g[j], rwkv_lnx_b[j]]))
        if j > 0:
            rw["v1"], rw["v2"] = v1[j - 1], v2[j - 1]
        w["rwkv"].append(rw)

    mem_rows = mem_prompt.reshape(b_p * MEM_TOKENS, D_MODEL)
    mem4 = lambda t: t.reshape(DEPTH, b_p, MEM_TOKENS, D_MODEL)
    mk, mk16 = map(mem4, _mem_kv(mem_rows, xa_w_k, "mem_k"))
    mv, mv16 = map(mem4, _mem_kv(mem_rows, xa_w_v, "mem_v"))
    z_s5 = jnp.zeros((n_s5, b_p, S5_GROUPS, S5_STATE), F32)
    z_rwkv = jnp.zeros((n_rwkv, b_p, RWKV_HEADS, RWKV_HEAD, RWKV_HEAD), F32)
    z_shift = jnp.zeros((n_rwkv, b_p, D_MODEL), F32)
    assert b_p == SCAN_SEQS
    y_p, re_p, im_p, s_p, sh_p = _trunk(x_prompt, b_p, t_p, mk16, mv16, z_s5, z_s5, z_rwkv, z_shift, w)

    y_s, re_s, im_s, s_s, sh_s = _trunk(_time_major(x_sample), b_s, t_s, cache_mem_k, cache_mem_v,
                                        state_s5_re, state_s5_im, state_rwkv, state_shift, w)

    kv_shape = (DEPTH, b_p, MEM_TOKENS, XA_HEADS, XA_HEAD_DIM)
    return (y_p, _batch_major(y_s, b_s, t_s), mk.reshape(kv_shape), mv.reshape(kv_shape),
            re_p, im_p, s_p, sh_p, re_s, im_s, s_s, sh_s)
```

```python
import functools

import jax
import jax.numpy as jnp
from jax import lax
from jax.experimental import pallas as pl
from jax.experimental.pallas import tpu as pltpu

F32 = jnp.float32
BF16 = jnp.bfloat16

D_MODEL = 1024
DEPTH = 4
S5_GROUP = 16
S5_GROUPS = 64
S5_STATE = 64
S5_WIDTH = S5_GROUPS * S5_STATE
S5_BLOCKS = 4
S5_BLOCK_IN = D_MODEL // S5_BLOCKS
S5_BLOCK_ST = S5_WIDTH // S5_BLOCKS
RWKV_HEADS = 16
RWKV_HEAD = 64
RWKV_PAIRS = RWKV_HEADS // 2
PAIR_W = 2 * RWKV_HEAD
RWKV_CHUNK = 64
RWKV_GN_EPS = 64e-5
SCAN_SEQS = 8
SCAN_LANES = 512
SCAN_PAIRS = SCAN_LANES // PAIR_W
SCAN_GROUP = 4
SHORT_GROUPS = 2
MEM_TOKENS = 256
XA_HEADS = 4
XA_HEAD_DIM = 256
D_FF = 4 * D_MODEL
FF_CHUNK = 1024
LN_EPS = 1e-5
ALPHA = (2.0 * DEPTH) ** 0.25

V7X_VMEM_LIMIT = 56 * 1024 * 1024
ROW_TILE = 512
XA_ROWS = 1024


def _dot(a, b):
    return jnp.dot(a, b, preferred_element_type=F32)


def _dot_nt(a, b):
    return lax.dot_general(a, b, (((1,), (1,)), ((), ())), preferred_element_type=F32)


def _layer_norm(z, g, b):
    mu = jnp.mean(z, axis=-1, keepdims=True)
    zc = z - mu
    var = jnp.mean(zc * zc, axis=-1, keepdims=True)
    return zc * lax.rsqrt(var + LN_EPS) * g + b


def _to_seq_major(x):
    return jnp.transpose(x.reshape(x.shape[0] // 8, 8, x.shape[1]), (1, 0, 2))


def _to_time_major(x3):
    return jnp.transpose(x3, (1, 0, 2)).reshape(x3.shape[0] * x3.shape[1], x3.shape[2])


def _load_rows(ref):
    return ref[...] if len(ref.shape) == 2 else _to_time_major(ref[...])


def _store_rows(ref, val):
    ref[...] = val if len(ref.shape) == 2 else _to_seq_major(val)


def _const_spec(shape):
    nd = len(shape)
    return pl.BlockSpec(tuple(shape), lambda *_: (0,) * nd, pipeline_mode=pl.Buffered(1))


def _layer_spec(stacked, layer):
    nd = stacked.ndim - 1
    return pl.BlockSpec((None,) + tuple(stacked.shape[1:]), lambda *_: (layer,) + (0,) * nd,
                        pipeline_mode=pl.Buffered(1))


def _params(*sem):
    return pltpu.CompilerParams(dimension_semantics=sem, vmem_limit_bytes=V7X_VMEM_LIMIT)


def _rows_call(body, name, rows, tm, row_ins, const_ins, outs, *, scratch=(), sequential=False,
               const_outs=(), seq_major_out=False):
    seq_spec = lambda c: pl.BlockSpec((SCAN_SEQS, tm // SCAN_SEQS, c), lambda i: (0, i, 0))
    in_specs = [seq_spec(a.shape[-1]) if a.ndim == 3 else pl.BlockSpec((tm, a.shape[-1]), lambda i: (i, 0))
                for a in row_ins]
    const_ins = [c if isinstance(c, tuple) else (c, None) for c in const_ins]
    in_specs += [_const_spec(a.shape) if layer is None else _layer_spec(a, layer) for a, layer in const_ins]
    if seq_major_out:
        out_specs = [seq_spec(c) for c, _ in outs]
        out_shape = [jax.ShapeDtypeStruct((SCAN_SEQS, rows // SCAN_SEQS, c), dt) for c, dt in outs]
    else:
        out_specs = [pl.BlockSpec((tm, c), lambda i: (i, 0)) for c, _ in outs]
        out_shape = [jax.ShapeDtypeStruct((rows, c), dt) for c, dt in outs]
    for shp, dt in const_outs:
        nd = len(shp)
        out_specs.append(pl.BlockSpec(tuple(shp), lambda i, nd=nd: (0,) * nd))
        out_shape.append(jax.ShapeDtypeStruct(tuple(shp), dt))
    return pl.pallas_call(
        body, grid=(rows // tm,), in_specs=in_specs, out_specs=out_specs, out_shape=out_shape,
        scratch_shapes=list(scratch),
        compiler_params=_params("arbitrary" if sequential else "parallel"), name=name,
    )(*row_ins, *[a for a, _ in const_ins])


def _matmul_body(x_ref, w_ref, o_ref):
    o_ref[...] = _dot(x_ref[...].astype(BF16), w_ref[...]).astype(o_ref.dtype)


def _matmul(x, w, name):
    rows = x.shape[0]
    return _rows_call(_matmul_body, name, rows, min(ROW_TILE, rows), [x], [w], [(w[0].shape[-1], F32)])[0]


def _memkv_body(x_ref, w_ref, o_ref, o16_ref):
    y = _dot(x_ref[...].astype(BF16), w_ref[0].astype(BF16))
    o_ref[0] = y
    o16_ref[0] = y.astype(BF16)


def _mem_kv(mem_rows, w_stack, name):
    rows = mem_rows.shape[0]
    out_spec = pl.BlockSpec((1, rows, D_MODEL), lambda l: (l, 0, 0))
    return pl.pallas_call(
        _memkv_body, grid=(DEPTH,),
        in_specs=[_const_spec(mem_rows.shape), pl.BlockSpec((1, D_MODEL, D_MODEL), lambda l: (l, 0, 0))],
        out_specs=[out_spec, out_spec],
        out_shape=[jax.ShapeDtypeStruct((DEPTH, rows, D_MODEL), F32),
                   jax.ShapeDtypeStruct((DEPTH, rows, D_MODEL), BF16)],
        compiler_params=_params("parallel"), name=name,
    )(mem_rows, w_stack)


def _proj_ln_body(x_ref, o_ref, w_ref, g_ref, b_ref, out_ref):
    h = _dot(o_ref[...].astype(BF16), w_ref[...])
    out_ref[...] = _layer_norm(ALPHA * x_ref[...] + h, g_ref[...], b_ref[...])


def _mlp_body(x_ref, w1_ref, w2_ref, g_ref, b_ref, out_ref):
    x = x_ref[...]
    xb = x.astype(BF16)
    acc = None
    for c in range(D_FF // FF_CHUNK):
        h = _dot(xb, w1_ref[:, c * FF_CHUNK:(c + 1) * FF_CHUNK])
        h = jnp.maximum(h, 0.0)
        p = _dot((h * h).astype(BF16), w2_ref[c * FF_CHUNK:(c + 1) * FF_CHUNK, :])
        acc = p if acc is None else acc + p
    _store_rows(out_ref, _layer_norm(ALPHA * x + acc, g_ref[...], b_ref[...]))


def _softmax_rows(s):
    m = jnp.max(s, axis=-1, keepdims=True)
    e = jnp.exp(s - m)
    return e / jnp.sum(e, axis=-1, keepdims=True)


def _xattn_fused_body(x_ref, wq_ref, k_ref, v_ref, wo_ref, g_ref, b_ref, out_ref):
    x = x_ref[...]
    q3 = _to_seq_major(_dot(x.astype(BF16), wq_ref[...])).astype(BF16)
    heads = [slice(h * XA_HEAD_DIM, (h + 1) * XA_HEAD_DIM) for h in range(XA_HEADS)]
    units = [(b, sl) for b in range(SCAN_SEQS) for sl in heads]
    s = [_dot_nt(q3[b][:, sl], k_ref[b][:, sl]) * (XA_HEAD_DIM ** -0.5) for b, sl in units]
    p = [_softmax_rows(t).astype(BF16) for t in s]
    pv = [_dot(p[u], v_ref[b][:, sl]) for u, (b, sl) in enumerate(units)]
    per_seq = [jnp.concatenate(pv[b * XA_HEADS:(b + 1) * XA_HEADS], axis=-1) for b in range(SCAN_SEQS)]
    o = _to_time_major(jnp.stack(per_seq, axis=0))
    out_ref[...] = _layer_norm(ALPHA * x + _dot(o.astype(BF16), wo_ref[...]), g_ref[...], b_ref[...])


def _xattn_fused(x, wq, mem_k, mem_v, wo, ln_g, ln_b, layer):
    rows = x.shape[0]
    tm = min(XA_ROWS, rows)
    tile = pl.BlockSpec((tm, D_MODEL), lambda i: (i, 0))
    kv_spec = pl.BlockSpec((None, SCAN_SEQS, MEM_TOKENS, D_MODEL), lambda i: (layer, 0, 0, 0),
                           pipeline_mode=pl.Buffered(1))
    vec = _const_spec((1, D_MODEL))
    return pl.pallas_call(
        _xattn_fused_body, grid=(rows // tm,),
        in_specs=[tile, _layer_spec(*wq), kv_spec, kv_spec, _layer_spec(*wo), vec, vec],
        out_specs=tile, out_shape=jax.ShapeDtypeStruct((rows, D_MODEL), F32),
        compiler_params=_params("parallel"), name="xattn_fused",
    )(x, wq[0], mem_k, mem_v, wo[0], ln_g, ln_b)


def _attn_cache_body(q_ref, k_ref, v_ref, o_ref):
    t_len, n_seq = q_ref.shape[0], q_ref.shape[1]
    shape = (XA_HEADS * t_len, MEM_TOKENS * XA_HEADS)
    own_head = (lax.broadcasted_iota(jnp.int32, shape, 0) // t_len
                == lax.broadcasted_iota(jnp.int32, shape, 1) % XA_HEADS)
    q3 = _to_seq_major(q_ref[...].reshape(t_len * n_seq, D_MODEL))
    outs = []
    for b in range(n_seq):
        q = q3[b]
        qs = jnp.concatenate([q[:, h * XA_HEAD_DIM:(h + 1) * XA_HEAD_DIM] for h in range(XA_HEADS)], axis=0)
        k2 = k_ref[b].reshape(MEM_TOKENS * XA_HEADS, XA_HEAD_DIM).astype(BF16)
        v2 = v_ref[b].reshape(MEM_TOKENS * XA_HEADS, XA_HEAD_DIM).astype(BF16)
        s = _dot_nt(qs.astype(BF16), k2) * (XA_HEAD_DIM ** -0.5)
        p = _softmax_rows(jnp.where(own_head, s, -1e30))
        o = _dot(p.astype(BF16), v2)
        outs.append(jnp.concatenate([o[h * t_len:(h + 1) * t_len] for h in range(XA_HEADS)], axis=-1))
    o_ref[...] = _to_time_major(jnp.stack(outs, axis=0)).reshape(t_len, n_seq, D_MODEL)


def _attention_cache(q_rows, cache_k, cache_v, layer, batch, t_len):
    nb = SCAN_SEQS
    assert t_len == 8 and batch % nb == 0
    kv_spec = pl.BlockSpec((None, nb, MEM_TOKENS, XA_HEADS, XA_HEAD_DIM), lambda b: (layer, b, 0, 0, 0))
    tile = pl.BlockSpec((t_len, nb, D_MODEL), lambda b: (0, b, 0))
    return pl.pallas_call(
        _attn_cache_body, grid=(batch // nb,), in_specs=[tile, kv_spec, kv_spec], out_specs=tile,
        out_shape=jax.ShapeDtypeStruct((t_len, batch, D_MODEL), F32),
        compiler_params=_params("parallel"), name="xattn_cache",
    )(q_rows.reshape(t_len, batch, D_MODEL), cache_k, cache_v).reshape(t_len * batch, D_MODEL)


def _s5_prep_body(lre_ref, lim_ref, ldt_ref, br_ref, bi_ref, are_ref, aim_ref, bbre_ref, bbim_ref):
    lre = lre_ref[...]
    lim = lim_ref[...]
    dt = jnp.exp(ldt_ref[...])
    mag = jnp.exp(lre * dt)
    ang = lim * dt
    ab_re = mag * jnp.cos(ang)
    ab_im = mag * jnp.sin(ang)
    den = lre * lre + lim * lim
    num_re = ab_re - 1.0
    f_re = (num_re * lre + ab_im * lim) / den
    f_im = (ab_im * lre - num_re * lim) / den
    br = br_ref[...]
    bi = bi_ref[...]
    are_ref[...] = ab_re
    aim_ref[...] = ab_im
    bbre_ref[...] = f_re * br - f_im * bi
    bbim_ref[...] = f_re * bi + f_im * br


def _s5_prep(a_re, a_im, log_dt, b_re, b_im, c_re, c_im):
    rep = lambda t: jnp.repeat(t, S5_GROUP, axis=0)
    to_rows = lambda t: jnp.transpose(t, (0, 2, 1)).reshape(S5_GROUPS * S5_GROUP, S5_STATE)
    ins = [rep(a_re), rep(a_im), rep(jnp.broadcast_to(log_dt[:, None], (S5_GROUPS, S5_STATE))),
           to_rows(b_re), to_rows(b_im)]
    shp = jax.ShapeDtypeStruct((S5_GROUPS * S5_GROUP, S5_STATE), F32)
    full = pl.BlockSpec((S5_GROUPS * S5_GROUP, S5_STATE), lambda: (0, 0))
    are, aim, bbre, bbim = pl.pallas_call(
        _s5_prep_body, in_specs=[full] * 5, out_specs=[full] * 4, out_shape=[shp] * 4, name="s5_prep",
    )(*ins)
    a_row = lambda t: t[::S5_GROUP].reshape(1, S5_WIDTH)
    eye = jnp.eye(S5_GROUP, dtype=F32)
    blk_b = lambda t: jnp.einsum('qgcn,gh->qgchn', t.reshape(S5_BLOCKS, 16, S5_GROUP, S5_STATE), eye
                                 ).reshape(S5_BLOCKS, S5_BLOCK_IN, S5_BLOCK_ST)
    wb = jnp.concatenate([blk_b(bbre), blk_b(bbim)], axis=2).astype(BF16)
    blk_c = lambda t: jnp.einsum('qgcn,gh->qhngc', t.reshape(S5_BLOCKS, 16, S5_GROUP, S5_STATE), eye
                                 ).reshape(S5_BLOCKS, S5_BLOCK_ST, S5_BLOCK_IN)
    wc = jnp.concatenate([blk_c(c_re), -blk_c(c_im)], axis=1).astype(BF16)
    return a_row(are), a_row(aim), wb, wc


def _s5_body(batch, x_ref, h0re_ref, h0im_ref, are_ref, aim_ref, wb_ref, wc_ref, d_ref, wv_ref, wg_ref,
             g_ref, b_ref, out_ref, hre_ref, him_ref, hbuf, sre, sim):
    x = _load_rows(x_ref)
    rows = x.shape[0]
    steps = rows // batch

    @pl.when(pl.program_id(0) == 0)
    def _():
        sre[...] = h0re_ref[...]
        sim[...] = h0im_ref[...]

    xb = x.astype(BF16)
    for q in range(S5_BLOCKS):
        hbuf[:, q * 2 * S5_BLOCK_ST:(q + 1) * 2 * S5_BLOCK_ST] = _dot(
            xb[:, q * S5_BLOCK_IN:(q + 1) * S5_BLOCK_IN], wb_ref[q])

    ys = []
    for q in range(S5_BLOCKS):
        st = slice(q * S5_BLOCK_ST, (q + 1) * S5_BLOCK_ST)
        c_re = slice(q * 2 * S5_BLOCK_ST, q * 2 * S5_BLOCK_ST + S5_BLOCK_ST)
        c_im = slice(q * 2 * S5_BLOCK_ST + S5_BLOCK_ST, (q + 1) * 2 * S5_BLOCK_ST)
        ar = jnp.broadcast_to(are_ref[:, st], (8, S5_BLOCK_ST))
        ai = jnp.broadcast_to(aim_ref[:, st], (8, S5_BLOCK_ST))
        for rg in range(batch // 8):
            rs = slice(rg * 8, rg * 8 + 8)
            hr, hi = sre[rs, st], sim[rs, st]
            for t in range(steps):
                row = slice(t * batch + rg * 8, t * batch + rg * 8 + 8)
                hr, hi = (ar * hr - ai * hi + hbuf[row, c_re], ar * hi + ai * hr + hbuf[row, c_im])
                hbuf[row, c_re] = hr
                hbuf[row, c_im] = hi
            sre[rs, st] = hr
            sim[rs, st] = hi
        ys.append(_dot(hbuf[:, q * 2 * S5_BLOCK_ST:(q + 1) * 2 * S5_BLOCK_ST].astype(BF16), wc_ref[q]))
    hre_ref[...] = sre[...]
    him_ref[...] = sim[...]
    y = jnp.concatenate(ys, axis=-1) + d_ref[...] * x
    z = jax.nn.gelu(y).astype(BF16)
    h = _dot(z, wv_ref[...]) * jax.nn.sigmoid(_dot(z, wg_ref[...]))
    out_ref[...] = _layer_norm(ALPHA * x + h, g_ref[...], b_ref[...])


def _s5_layer(x, batch, h0re, h0im, s5w, ln_g, ln_b):
    rows = x.size // D_MODEL
    tm = min(ROW_TILE, rows)
    a_re, a_im, wb, wc, d_vec, wv, wg = s5w
    out, hre, him = _rows_call(
        functools.partial(_s5_body, batch), "s5_mixer", rows, tm, [x],
        [h0re, h0im, a_re, a_im, wb, wc, d_vec, wv, wg, ln_g, ln_b], [(D_MODEL, F32)],
        scratch=[pltpu.VMEM((tm, 2 * S5_WIDTH), F32), pltpu.VMEM((batch, S5_WIDTH), F32),
                 pltpu.VMEM((batch, S5_WIDTH), F32)],
        sequential=True, const_outs=[((batch, S5_WIDTH), F32)] * 2)
    return out, hre, him


def _head_sum(v, ones_ref):
    return _dot(v.astype(BF16), ones_ref[...])


def _token_shift_mix(batch, x_ref, shift_ref, mu_ref, xprev):
    rows = x_ref.shape[0]

    @pl.when(pl.program_id(0) == 0)
    def _():
        xprev[...] = shift_ref[...]

    x = x_ref[...]
    if rows > batch:
        xs = jnp.concatenate([xprev[...], x[:rows - batch]], axis=0)
    else:
        xs = xprev[...]
    xprev[...] = x[rows - batch:]
    xx = xs - x
    return lambda i: (x + xx * mu_ref[i:i + 1, :]).astype(BF16)


def _rwkv_proj_rk_body(batch, x_ref, shift_ref, mu_ref, wr_ref, wk_ref, a1_ref, a2_ref, vec_ref, ones_ref,
                       r_out, k_out, a_out, b_out, xprev):
    mix = _token_shift_mix(batch, x_ref, shift_ref, mu_ref, xprev)
    a0, k_k, k_a = vec_ref[1:2, :], vec_ref[3:4, :], vec_ref[4:5, :]
    _store_rows(r_out, _dot(mix(0), wr_ref[...]))
    k = _dot(mix(2), wk_ref[...])
    a = jax.nn.sigmoid(a0 + _dot(_dot(mix(4), a1_ref[...]).astype(BF16), a2_ref[...]))
    kk = k * k_k
    sq = kk * kk
    sq_hi = sq.astype(BF16)
    ss = _dot(sq_hi, ones_ref[...]) + _head_sum(sq - sq_hi.astype(F32), ones_ref)
    kk = kk / jnp.maximum(jnp.sqrt(ss), 1e-12)
    _store_rows(k_out, k * (1.0 + (a - 1.0) * k_a))
    _store_rows(a_out, -kk)
    _store_rows(b_out, kk * a)


def _rwkv_proj_vwg_body(batch, has_vgate, *refs):
    if has_vgate:
        (x_ref, vf_ref, shift_ref, mu_ref, wv_ref, w1_ref, w2_ref, g1_ref, g2_ref, v1_ref, v2_ref, vec_ref,
         lw_out, v_out, g_out, xprev) = refs
    else:
        (x_ref, shift_ref, mu_ref, wv_ref, w1_ref, w2_ref, g1_ref, g2_ref, vec_ref,
         lw_out, v_out, g_out, xprev) = refs
    mix = _token_shift_mix(batch, x_ref, shift_ref, mu_ref, xprev)
    w0, v0 = vec_ref[0:1, :], vec_ref[2:3, :]
    xv = mix(3)
    v = _dot(xv, wv_ref[...])
    w_lora = _dot(jnp.tanh(_dot(mix(1), w1_ref[...])).astype(BF16), w2_ref[...])
    w_log = -jax.nn.softplus(-(w0 + w_lora)) - 0.5
    _store_rows(lw_out, -jnp.exp(w_log))
    if has_vgate:
        gate_v = jax.nn.sigmoid(v0 + _dot(_dot(xv, v1_ref[...]).astype(BF16), v2_ref[...]))
        v = v + (_load_rows(vf_ref) - v) * gate_v
    _store_rows(v_out, v)
    _store_rows(g_out, _dot(jax.nn.sigmoid(_dot(mix(5), g1_ref[...])).astype(BF16), g2_ref[...]))


def _rwkv_consts(seq_len=RWKV_CHUNK):
    L = RWKV_CHUNK
    lane = lax.broadcasted_iota(jnp.int32, (1, PAIR_W), 1)
    t_idx = lax.broadcasted_iota(jnp.int32, (L, PAIR_W), 0)
    s_idx = lax.broadcasted_iota(jnp.int32, (L, PAIR_W), 1) % L
    r2 = lax.broadcasted_iota(jnp.int32, (2 * L, PAIR_W), 0)
    c2 = lax.broadcasted_iota(jnp.int32, (2 * L, PAIR_W), 1)
    rr = lax.broadcasted_iota(jnp.int32, (L, L), 0)
    cc = lax.broadcasted_iota(jnp.int32, (L, L), 1)
    strict_lower, incl_lower, tri = s_idx < t_idx, s_idx <= t_idx, cc <= rr
    cst = {}
    if seq_len < L:
        same_seq = (s_idx // seq_len) == (t_idx // seq_len)
        strict_lower = same_seq & strict_lower
        incl_lower = same_seq & incl_lower
        tri = ((cc // seq_len) == (rr // seq_len)) & tri
        cst["seq_ones"] = jnp.where((cc // seq_len) == (rr // seq_len), 1.0, 0.0).astype(BF16)
        cst["col_seq"] = (c2 % L) // seq_len
    cst.update(
        m0=lane < RWKV_HEAD,
        strict_lower=strict_lower, incl_lower=incl_lower,
        same_head=(r2 < L) == (c2 < RWKV_HEAD),
        eye2=jnp.where(r2 == c2, 1.0, 0.0).astype(F32),
        tri=jnp.where(tri, 1.0, 0.0).astype(BF16),
        zeros_l=jnp.zeros((L, PAIR_W), F32))
    return cst


def _expand_state(sc, m0):
    return jnp.concatenate([jnp.where(m0, sc, 0.0), jnp.where(m0, 0.0, sc)], axis=0)


def _compact_state(s_bd):
    return s_bd[0:RWKV_HEAD] + s_bd[RWKV_HEAD:PAIR_W]


def _rwkv_chunk(slabs, states, cst, seq_len=RWKV_CHUNK):
    L = RWKV_CHUNK
    n_seq = L // seq_len
    m0 = cst["m0"]
    bf = lambda t: t.astype(BF16)
    cat = lambda ts: jnp.concatenate(ts, axis=0)
    ap, rp, vp, btp, ktp, wbp, wkp, dtot = [], [], [], [], [], [], [], []
    for r, lw, k, v, a, b in slabs:
        h1 = bf(lw)
        e1 = lw - h1.astype(F32)
        h2 = bf(e1)
        h3 = bf(e1 - h2.astype(F32))
        cum = _dot(cst["tri"], h1) + _dot(cst["tri"], h2) + _dot(cst["tri"], h3)
        if n_seq == 1:
            tot = cum[L - 1:L, :]
        else:
            tot = _dot(cst["seq_ones"], h1) + _dot(cst["seq_ones"], h2) + _dot(cst["seq_ones"], h3)
        inv = jnp.exp(-cum)
        tail = jnp.exp(tot - cum)
        pieces = (a * jnp.exp(cum - lw), r * jnp.exp(cum), v, b * inv, k * inv, b * tail, k * tail,
                  jnp.exp(tot))
        for p in range(r.shape[1] // PAIR_W):
            sl = slice(p * PAIR_W, (p + 1) * PAIR_W)
            for dst, src in zip((ap, rp, vp, btp, ktp, wbp, wkp, dtot), pieces):
                dst.append(src[:, sl])
    P = range(len(ap))
    gram = [_dot_nt(bf(cat([ap[p], rp[p]])),
                    bf(cat([jnp.where(m0, btp[p], 0.0), jnp.where(m0, ktp[p], 0.0),
                            jnp.where(m0, 0.0, btp[p]), jnp.where(m0, 0.0, ktp[p])]))) for p in P]
    if n_seq == 1:
        ar_s = [_dot_nt(bf(cat([ap[p], rp[p]])), bf(states[p])) for p in P]
    else:
        ar_s = []
        for p in P:
            parts = [_dot_nt(bf(cat([ap[p][s * seq_len:(s + 1) * seq_len], rp[p][s * seq_len:(s + 1) * seq_len]])),
                             bf(states[p * n_seq + s])) for s in range(n_seq)]
            ar_s.append(cat([o[0:seq_len] for o in parts] + [o[seq_len:2 * seq_len] for o in parts]))
    na0 = [jnp.where(cst["strict_lower"], gram[p][0:L, 0:PAIR_W], 0.0) for p in P]
    na1 = [jnp.where(cst["strict_lower"], gram[p][0:L, PAIR_W:2 * PAIR_W], 0.0) for p in P]
    nr = [cat([jnp.where(cst["incl_lower"], gram[p][L:2 * L, 0:PAIR_W], 0.0),
               jnp.where(cst["incl_lower"], gram[p][L:2 * L, PAIR_W:2 * PAIR_W], 0.0)]) for p in P]
    nv = [_dot(bf(cat([na0[p], na1[p]])), bf(cat([cst["zeros_l"], vp[p]]))) for p in P]
    pw = [cat([jnp.where(m0, na0[p], 0.0), jnp.where(m0, 0.0, pltpu.roll(na1[p], RWKV_HEAD, axis=1))])
          for p in P]
    t_inv = [cst["eye2"] + pw[p] for p in P]
    for _ in range(seq_len.bit_length() - 2):
        pwb = [bf(pw[p]) for p in P]
        pw = [_dot(pwb[p], pwb[p]) for p in P]
        t_inv = [t_inv[p] + _dot(bf(t_inv[p]), bf(pw[p])) for p in P]
    u_st = [_dot(bf(t_inv[p]), bf(cat([ar_s[p][0:L], ar_s[p][0:L]]) + nv[p])) for p in P]
    zz = [cat([jnp.where(m0, u_st[p][0:L], u_st[p][L:2 * L]), vp[p]]) for p in P]
    y_st = [_dot(bf(nr[p]), bf(zz[p])) for p in P]
    ys = [ar_s[p][L:2 * L] + jnp.where(m0, y_st[p][0:L], y_st[p][L:2 * L]) for p in P]
    if n_seq == 1:
        upd = [_dot(bf(zz[p].T), bf(cat([wbp[p], wkp[p]]))) for p in P]
        new_states = [states[p] * dtot[p] + jnp.where(cst["same_head"], upd[p], 0.0) for p in P]
    else:
        new_states = []
        for p in P:
            zz_t = zz[p].T
            lhs = cat([jnp.where(cst["col_seq"] == s, zz_t, 0.0) for s in range(n_seq)])
            upd = _dot(bf(lhs), bf(cat([wbp[p], wkp[p]])))
            for s in range(n_seq):
                new_states.append(states[p * n_seq + s] * dtot[p][s * seq_len:s * seq_len + 1]
                                  + jnp.where(cst["same_head"], upd[s * PAIR_W:(s + 1) * PAIR_W], 0.0))
    y_slabs, at = [], 0
    for slab in slabs:
        n_i = slab[0].shape[1] // PAIR_W
        y_slabs.append(jnp.concatenate(ys[at:at + n_i], axis=1) if n_i > 1 else ys[at])
        at += n_i
    return y_slabs, new_states


def _rwkv_scan_short_body(r_ref, lw_ref, k_ref, v_ref, a_ref, b_ref, s0_ref, y_ref, sout_ref):
    t_len, n_seq = r_ref.shape[0], SCAN_SEQS
    cst = _rwkv_consts(t_len)
    slabs, states = [], []
    for g in range(SHORT_GROUPS):
        rows = lambda ref: _to_seq_major(ref[:, g * n_seq:(g + 1) * n_seq, :].reshape(t_len * n_seq, D_MODEL)
                                         ).reshape(n_seq * t_len, D_MODEL)
        slabs.append(tuple(rows(ref) for ref in (r_ref, lw_ref, k_ref, v_ref, a_ref, b_ref)))
        states += [_expand_state(s0_ref[g * n_seq + s, p], cst["m0"])
                   for p in range(RWKV_PAIRS) for s in range(n_seq)]
    ys, new_states = _rwkv_chunk(slabs, states, cst, t_len)
    for g in range(SHORT_GROUPS):
        y_ref[:, g * n_seq:(g + 1) * n_seq, :] = _to_time_major(
            ys[g].reshape(n_seq, t_len, D_MODEL)).reshape(t_len, n_seq, D_MODEL)
        for s in range(n_seq):
            for p in range(RWKV_PAIRS):
                sout_ref[g * n_seq + s, p] = _compact_state(new_states[(g * RWKV_PAIRS + p) * n_seq + s])


def _rwkv_scan_seq_body(r_ref, lw_ref, k_ref, v_ref, a_ref, b_ref, s0_ref, y_ref, sout_ref, s_scr):
    cst = _rwkv_consts()
    in_refs = (r_ref, lw_ref, k_ref, v_ref, a_ref, b_ref)

    @pl.when(pl.program_id(1) == 0)
    def _():
        for b in range(SCAN_SEQS):
            for p in range(SCAN_PAIRS):
                s_scr[b * SCAN_PAIRS + p] = _expand_state(s0_ref[b, p], cst["m0"])

    def sequence_group(i, carry):
        seqs = tuple(SCAN_GROUP * i + n for n in range(SCAN_GROUP))
        slabs = [tuple(ref[s] for ref in in_refs) for s in seqs]
        states = [s_scr[s * SCAN_PAIRS + p] for s in seqs for p in range(SCAN_PAIRS)]
        y_slabs, new_states = _rwkv_chunk(slabs, states, cst)
        for n, s in enumerate(seqs):
            y_ref[s] = y_slabs[n]
            for p in range(SCAN_PAIRS):
                s_scr[s * SCAN_PAIRS + p] = new_states[n * SCAN_PAIRS + p]
        return carry

    lax.fori_loop(0, SCAN_SEQS // SCAN_GROUP, sequence_group, 0)

    @pl.when(pl.program_id(1) == pl.num_programs(1) - 1)
    def _():
        for b in range(SCAN_SEQS):
            for p in range(SCAN_PAIRS):
                sout_ref[b, p] = _compact_state(s_scr[b * SCAN_PAIRS + p])


def _rwkv_out_body(y_ref, r_ref, k_ref, v_ref, g_ref, x_ref, ones_ref, vec_ref, wo_ref, lg_ref, lb_ref, out_ref):
    per_seq = len(y_ref.shape) == 3
    flat = lambda ref: ref[...].reshape(-1, D_MODEL) if per_seq else ref[...]
    x = x_ref[...]
    if per_seq:
        x = _to_seq_major(x).reshape(x.shape)
    y = flat(y_ref)
    r_k, lnx_g, lnx_b = (vec_ref[i:i + 1, :] for i in range(3))
    inv_n = 1.0 / RWKV_HEAD
    mu = _head_sum(y, ones_ref) * inv_n
    yc = y - mu
    var = _head_sum(yc * yc, ones_ref) * inv_n
    yn = yc * lax.rsqrt(var + RWKV_GN_EPS) * lnx_g + lnx_b
    bonus = _head_sum(flat(r_ref) * flat(k_ref) * r_k, ones_ref) * flat(v_ref)
    o = ((yn + bonus) * flat(g_ref)).astype(BF16)
    res = _layer_norm(ALPHA * x + _dot(o, wo_ref[...]), lg_ref[...], lb_ref[...])
    out_ref[...] = _to_time_major(res.reshape(y_ref.shape)) if per_seq else res


def _pair_state(s):
    bsz = s.shape[0]
    s = s.reshape(bsz, RWKV_PAIRS, 2, RWKV_HEAD, RWKV_HEAD)
    return jnp.transpose(s, (0, 1, 3, 2, 4)).reshape(bsz, RWKV_PAIRS, RWKV_HEAD, PAIR_W)


def _unpair_state(s):
    bsz = s.shape[0]
    s = s.reshape(bsz, RWKV_PAIRS, RWKV_HEAD, 2, RWKV_HEAD)
    return jnp.transpose(s, (0, 1, 3, 2, 4)).reshape(bsz, RWKV_HEADS, RWKV_HEAD, RWKV_HEAD)


def _rwkv_layer(x, batch, t_len, s0, shift0, v_first, rw, ln_g, ln_b):
    rows = x.shape[0]
    tm = min(ROW_TILE, rows)
    has_vgate = v_first is not None
    per_seq = batch == SCAN_SEQS
    shift_scratch = [pltpu.VMEM((batch, D_MODEL), F32)]
    r, k, a, b = _rows_call(
        functools.partial(_rwkv_proj_rk_body, batch), "rwkv_proj_rk", rows, tm, [x],
        [shift0, rw["mu"], rw["w_r"], rw["w_k"], rw["a1"], rw["a2"], rw["proj_vec"], rw["ones"]],
        [(D_MODEL, F32)] * 4, scratch=shift_scratch, sequential=True, seq_major_out=per_seq)
    consts = [shift0, rw["mu"], rw["w_v"], rw["w1"], rw["w2"], rw["g1"], rw["g2"]]
    if has_vgate:
        consts += [rw["v1"], rw["v2"]]
    lw, v, g = _rows_call(
        functools.partial(_rwkv_proj_vwg_body, batch, has_vgate), "rwkv_proj_vwg", rows, tm,
        [x] + ([v_first] if has_vgate else []), consts + [rw["proj_vec"]],
        [(D_MODEL, F32)] * 3, scratch=shift_scratch, sequential=True, seq_major_out=per_seq)

    if per_seq:
        assert t_len % RWKV_CHUNK == 0
        n_half = D_MODEL // SCAN_LANES
        tile = pl.BlockSpec((SCAN_SEQS, RWKV_CHUNK, SCAN_LANES), lambda h, c: (0, c, h))
        st_spec = pl.BlockSpec((batch, None, SCAN_PAIRS, RWKV_HEAD, PAIR_W), lambda h, c: (0, h, 0, 0, 0))
        st_shape = (batch, n_half, SCAN_PAIRS, RWKV_HEAD, PAIR_W)
        y, s_last = pl.pallas_call(
            _rwkv_scan_seq_body, grid=(n_half, t_len // RWKV_CHUNK),
            in_specs=[tile] * 6 + [st_spec], out_specs=[tile, st_spec],
            out_shape=[jax.ShapeDtypeStruct((batch, t_len, D_MODEL), F32), jax.ShapeDtypeStruct(st_shape, F32)],
            scratch_shapes=[pltpu.VMEM((SCAN_SEQS * SCAN_PAIRS, PAIR_W, PAIR_W), F32)],
            compiler_params=_params("arbitrary", "arbitrary"), name="rwkv_scan",
        )(r, lw, k, v, a, b, _pair_state(s0).reshape(st_shape))
        s_last = s_last.reshape(batch, RWKV_PAIRS, RWKV_HEAD, PAIR_W)
    else:
        n_seq = SCAN_SEQS * SHORT_GROUPS
        assert t_len * SCAN_SEQS == RWKV_CHUNK and batch % n_seq == 0
        view = lambda t: t.reshape(t_len, batch, D_MODEL)
        tile = pl.BlockSpec((t_len, n_seq, D_MODEL), lambda bi: (0, bi, 0))
        st_spec = pl.BlockSpec((n_seq, RWKV_PAIRS, RWKV_HEAD, PAIR_W), lambda bi: (bi, 0, 0, 0))
        y2, s_last = pl.pallas_call(
            _rwkv_scan_short_body, grid=(batch // n_seq,),
            in_specs=[tile] * 6 + [st_spec], out_specs=[tile, st_spec],
            out_shape=[jax.ShapeDtypeStruct((t_len, batch, D_MODEL), F32),
                       jax.ShapeDtypeStruct((batch, RWKV_PAIRS, RWKV_HEAD, PAIR_W), F32)],
            compiler_params=_params("parallel"), name="rwkv_scan_short",
        )(view(r), view(lw), view(k), view(v), view(a), view(b), _pair_state(s0))
        y = y2.reshape(rows, D_MODEL)

    out = _rows_call(_rwkv_out_body, "rwkv_out", rows, tm, [y, r, k, v, g, x],
                     [rw["ones"], rw["out_vec"], rw["w_o"], ln_g, ln_b], [(D_MODEL, F32)])[0]
    return out, _unpair_state(s_last), x[rows - batch:], (v if not has_vgate else v_first)


def _trunk(x, batch, t_len, mem_k, mem_v, s5_re, s5_im, rwkv_s, shift, w):
    rows = x.size // D_MODEL
    tm = min(ROW_TILE, rows)
    new_re, new_im, new_s, new_shift = [], [], [], []
    v_first = None
    for i in range(DEPTH):
        j = i // 2
        ln = lambda n: (w["ln_g"][i, n][None, :], w["ln_b"][i, n][None, :])
        if i % 2 == 0:
            x, hre, him = _s5_layer(x, batch, s5_re[j].reshape(batch, S5_WIDTH),
                                    s5_im[j].reshape(batch, S5_WIDTH), w["s5"][j], *ln(0))
            new_re.append(hre.reshape(batch, S5_GROUPS, S5_STATE))
            new_im.append(him.reshape(batch, S5_GROUPS, S5_STATE))
        else:
            x, s_last, x_last, v_first = _rwkv_layer(x, batch, t_len, rwkv_s[j], shift[j], v_first,
                                                     w["rwkv"][j], *ln(0))
            new_s.append(s_last)
            new_shift.append(x_last)
        if batch == SCAN_SEQS:
            x = _xattn_fused(x, w["xa_q"][i], mem_k, mem_v, w["xa_o"][i], *ln(1), i)
        else:
            o = _attention_cache(_matmul(x, w["xa_q"][i], "xattn_q"), mem_k, mem_v, i, batch, t_len)
            x = _rows_call(_proj_ln_body, "xattn_out", rows, tm, [x, o],
                           [w["xa_o"][i], *ln(1)], [(D_MODEL, F32)])[0]
        x = _rows_call(_mlp_body, "mlp", rows, tm, [x], [w["mlp1"][i], w["mlp2"][i], *ln(2)], [(D_MODEL, F32)],
                       seq_major_out=(batch == SCAN_SEQS and i == DEPTH - 1))[0]
    return x, jnp.stack(new_re), jnp.stack(new_im), jnp.stack(new_s), jnp.stack(new_shift)


def _time_major(x):
    bsz, t_len, d = x.shape
    return jnp.transpose(x, (1, 0, 2)).reshape(t_len * bsz, d)


def _batch_major(x, bsz, t_len):
    return jnp.transpose(x.reshape(t_len, bsz, -1), (1, 0, 2))


def kernel(x_prompt, x_sample, mem_prompt, cache_mem_k, cache_mem_v, state_s5_re, state_s5_im, state_rwkv, state_shift, ln_g, ln_b, s5_a_re, s5_a_im, s5_log_dt, s5_b_re, s5_b_im, s5_c_re, s5_c_im, s5_d, s5_w_glu_v, s5_w_glu_g, rwkv_mu, rwkv_w_r, rwkv_w_k, rwkv_w_v, rwkv_w_o, rwkv_w0, rwkv_w1, rwkv_w2, rwkv_a0, rwkv_a1, rwkv_a2, rwkv_v0, rwkv_v1, rwkv_v2, rwkv_g1, rwkv_g2, rwkv_k_k, rwkv_k_a, rwkv_r_k, rwkv_lnx_g, rwkv_lnx_b, xa_w_q, xa_w_k, xa_w_v, xa_w_o, mlp_w1, mlp_w2):
    bf = lambda t: t.astype(BF16)
    n_s5, n_rwkv = state_s5_re.shape[0], state_rwkv.shape[0]
    b_p, t_p = x_prompt.shape[0], x_prompt.shape[1]
    b_s, t_s = x_sample.shape[0], x_sample.shape[1]

    head_id = jnp.arange(D_MODEL) // RWKV_HEAD
    ones_bd = (head_id[:, None] == head_id[None, :]).astype(BF16)
    per_layer = lambda stack, n: [(s16, i) for s16 in [bf(stack)] for i in range(n)]
    w = {"ln_g": ln_g, "ln_b": ln_b, "s5": [], "rwkv": [],
         "xa_q": per_layer(xa_w_q, DEPTH), "xa_o": per_layer(xa_w_o, DEPTH),
         "mlp1": per_layer(mlp_w1, DEPTH), "mlp2": per_layer(mlp_w2, DEPTH)}
    glu_v, glu_g = per_layer(s5_w_glu_v, n_s5), per_layer(s5_w_glu_g, n_s5)
    for j in range(n_s5):
        a_re, a_im, wb, wc = _s5_prep(s5_a_re[j], s5_a_im[j], s5_log_dt[j], s5_b_re[j], s5_b_im[j],
                                      s5_c_re[j], s5_c_im[j])
        w["s5"].append((a_re, a_im, wb, wc, s5_d[j][None, :], glu_v[j], glu_g[j]))
    stacks = {name: per_layer(t, n_rwkv) for name, t in (
        ("w_r", rwkv_w_r), ("w_k", rwkv_w_k), ("w_v", rwkv_w_v), ("w_o", rwkv_w_o), ("w1", rwkv_w1),
        ("w2", rwkv_w2), ("a1", rwkv_a1), ("a2", rwkv_a2), ("g1", rwkv_g1), ("g2", rwkv_g2))}
    v1, v2 = per_layer(rwkv_v1, n_rwkv - 1), per_layer(rwkv_v2, n_rwkv - 1)
    for j in range(n_rwkv):
        v0 = rwkv_v0[j - 1] if j > 0 else jnp.zeros((D_MODEL,), F32)
        rw = {name: stack[j] for name, stack in stacks.items()}
        rw.update(mu=rwkv_mu[j], ones=ones_bd,
                  proj_vec=jnp.stack([rwkv_w0[j], rwkv_a0[j], v0, rwkv_k_k[j], rwkv_k_a[j]]),
                  out_vec=jnp.stack([rwkv_r_k[j].reshape(D_MODEL), rwkv_lnx_g[j], rwkv_lnx_b[j]]))
        if j > 0:
            rw["v1"], rw["v2"] = v1[j - 1], v2[j - 1]
        w["rwkv"].append(rw)

    mem_rows = mem_prompt.reshape(b_p * MEM_TOKENS, D_MODEL)
    mem4 = lambda t: t.reshape(DEPTH, b_p, MEM_TOKENS, D_MODEL)
    mk, mk16 = map(mem4, _mem_kv(mem_rows, xa_w_k, "mem_k"))
    mv, mv16 = map(mem4, _mem_kv(mem_rows, xa_w_v, "mem_v"))
    z_s5 = jnp.zeros((n_s5, b_p, S5_GROUPS, S5_STATE), F32)
    z_rwkv = jnp.zeros((n_rwkv, b_p, RWKV_HEADS, RWKV_HEAD, RWKV_HEAD), F32)
    z_shift = jnp.zeros((n_rwkv, b_p, D_MODEL), F32)
    assert b_p == SCAN_SEQS
    y_p, re_p, im_p, s_p, sh_p = _trunk(x_prompt, b_p, t_p, mk16, mv16, z_s5, z_s5, z_rwkv, z_shift, w)

    y_s, re_s, im_s, s_s, sh_s = _trunk(_time_major(x_sample), b_s, t_s, cache_mem_k, cache_mem_v,
                                        state_s5_re, state_s5_im, state_rwkv, state_shift, w)

    kv_shape = (DEPTH, b_p, MEM_TOKENS, XA_HEADS, XA_HEAD_DIM)
    return (y_p, _batch_major(y_s, b_s, t_s), mk.reshape(kv_shape), mv.reshape(kv_shape),
            re_p, im_p, s_p, sh_p, re_s, im_s, s_s, sh_s)
```

```python
import functools

import jax
import jax.numpy as jnp
from jax import lax
from jax.experimental import pallas as pl
from jax.experimental.pallas import tpu as pltpu

F32 = jnp.float32
BF16 = jnp.bfloat16

D_MODEL = 1024
DEPTH = 4
S5_GROUP = 16
S5_GROUPS = 64
S5_STATE = 64
S5_WIDTH = S5_GROUPS * S5_STATE
S5_BLOCKS = 4
S5_BLOCK_IN = D_MODEL // S5_BLOCKS
S5_BLOCK_ST = S5_WIDTH // S5_BLOCKS
RWKV_HEADS = 16
RWKV_HEAD = 64
RWKV_PAIRS = RWKV_HEADS // 2
PAIR_W = 2 * RWKV_HEAD
RWKV_CHUNK = 64
RWKV_GN_EPS = 64e-5
SCAN_SEQS = 8
SCAN_LANES = 1024
SCAN_PAIRS = SCAN_LANES // PAIR_W
SCAN_GROUP = 2
SHORT_GROUPS = 2
MEM_TOKENS = 256
XA_HEADS = 4
XA_HEAD_DIM = 256
D_FF = 4 * D_MODEL
FF_CHUNK = 1024
LN_EPS = 1e-5
ALPHA = (2.0 * DEPTH) ** 0.25

V7X_VMEM_LIMIT = 56 * 1024 * 1024
ROW_TILE = 512
XA_ROWS = 1024


def _dot(a, b):
    return jnp.dot(a, b, preferred_element_type=F32)


def _dot_nt(a, b):
    return lax.dot_general(a, b, (((1,), (1,)), ((), ())), preferred_element_type=F32)


def _layer_norm(z, g, b):
    mu = jnp.mean(z, axis=-1, keepdims=True)
    zc = z - mu
    var = jnp.mean(zc * zc, axis=-1, keepdims=True)
    return zc * lax.rsqrt(var + LN_EPS) * g + b


def _row_halves(rows):
    return (slice(0, rows // 2), slice(rows // 2, rows))


def _to_seq_major(x):
    return jnp.transpose(x.reshape(x.shape[0] // 8, 8, x.shape[1]), (1, 0, 2))


def _to_time_major(x3):
    return jnp.transpose(x3, (1, 0, 2)).reshape(x3.shape[0] * x3.shape[1], x3.shape[2])


def _load_rows(ref):
    return ref[...] if len(ref.shape) == 2 else _to_time_major(ref[...])


def _store_rows(ref, val):
    ref[...] = val if len(ref.shape) == 2 else _to_seq_major(val)


def _const_spec(shape):
    nd = len(shape)
    return pl.BlockSpec(tuple(shape), lambda *_: (0,) * nd, pipeline_mode=pl.Buffered(1))


def _layer_spec(stacked, layer):
    nd = stacked.ndim - 1
    return pl.BlockSpec((None,) + tuple(stacked.shape[1:]), lambda *_: (layer,) + (0,) * nd,
                        pipeline_mode=pl.Buffered(1))


def _params(*sem):
    return pltpu.CompilerParams(dimension_semantics=sem, vmem_limit_bytes=V7X_VMEM_LIMIT)


def _rows_call(body, name, rows, tm, row_ins, const_ins, outs, *, scratch=(), sequential=False,
               const_outs=(), seq_major_out=False):
    seq_spec = lambda c: pl.BlockSpec((SCAN_SEQS, tm // SCAN_SEQS, c), lambda i: (0, i, 0))
    in_specs = [seq_spec(a.shape[-1]) if a.ndim == 3 else pl.BlockSpec((tm, a.shape[-1]), lambda i: (i, 0))
                for a in row_ins]
    const_ins = [c if isinstance(c, tuple) else (c, None) for c in const_ins]
    in_specs += [_const_spec(a.shape) if layer is None else _layer_spec(a, layer) for a, layer in const_ins]
    if seq_major_out:
        out_specs = [seq_spec(c) for c, _ in outs]
        out_shape = [jax.ShapeDtypeStruct((SCAN_SEQS, rows // SCAN_SEQS, c), dt) for c, dt in outs]
    else:
        out_specs = [pl.BlockSpec((tm, c), lambda i: (i, 0)) for c, _ in outs]
        out_shape = [jax.ShapeDtypeStruct((rows, c), dt) for c, dt in outs]
    for shp, dt in const_outs:
        nd = len(shp)
        out_specs.append(pl.BlockSpec(tuple(shp), lambda i, nd=nd: (0,) * nd))
        out_shape.append(jax.ShapeDtypeStruct(tuple(shp), dt))
    return pl.pallas_call(
        body, grid=(rows // tm,), in_specs=in_specs, out_specs=out_specs, out_shape=out_shape,
        scratch_shapes=list(scratch),
        compiler_params=_params("arbitrary" if sequential else "parallel"), name=name,
    )(*row_ins, *[a for a, _ in const_ins])


def _matmul_body(x_ref, w_ref, o_ref):
    o_ref[...] = _dot(x_ref[...].astype(BF16), w_ref[...]).astype(o_ref.dtype)


def _matmul(x, w, name):
    rows = x.shape[0]
    return _rows_call(_matmul_body, name, rows, min(ROW_TILE, rows), [x], [w], [(w[0].shape[-1], F32)])[0]


def _memkv_body(x_ref, w_ref, o_ref, o16_ref):
    y = _dot(x_ref[...].astype(BF16), w_ref[0].astype(BF16))
    o_ref[0] = y
    o16_ref[0] = y.astype(BF16)


def _mem_kv(mem_rows, w_stack, name):
    rows = mem_rows.shape[0]
    out_spec = pl.BlockSpec((1, rows, D_MODEL), lambda l: (l, 0, 0))
    return pl.pallas_call(
        _memkv_body, grid=(DEPTH,),
        in_specs=[_const_spec(mem_rows.shape), pl.BlockSpec((1, D_MODEL, D_MODEL), lambda l: (l, 0, 0))],
        out_specs=[out_spec, out_spec],
        out_shape=[jax.ShapeDtypeStruct((DEPTH, rows, D_MODEL), F32),
                   jax.ShapeDtypeStruct((DEPTH, rows, D_MODEL), BF16)],
        compiler_params=_params("parallel"), name=name,
    )(mem_rows, w_stack)


def _proj_ln_body(x_ref, o_ref, w_ref, g_ref, b_ref, out_ref):
    h = _dot(o_ref[...].astype(BF16), w_ref[...])
    out_ref[...] = _layer_norm(ALPHA * x_ref[...] + h, g_ref[...], b_ref[...])


def _mlp_body(x_ref, w1_ref, w2_ref, g_ref, b_ref, out_ref):
    x = x_ref[...]
    xb = x.astype(BF16)
    acc = None
    for c in range(D_FF // FF_CHUNK):
        h = _dot(xb, w1_ref[:, c * FF_CHUNK:(c + 1) * FF_CHUNK])
        h = jnp.maximum(h, 0.0)
        p = _dot((h * h).astype(BF16), w2_ref[c * FF_CHUNK:(c + 1) * FF_CHUNK, :])
        acc = p if acc is None else acc + p
    _store_rows(out_ref, _layer_norm(ALPHA * x + acc, g_ref[...], b_ref[...]))


def _softmax_rows(s):
    m = jnp.max(s, axis=-1, keepdims=True)
    e = jnp.exp(s - m)
    return e / jnp.sum(e, axis=-1, keepdims=True)


def _xattn_fused_body(x_ref, wq_ref, k_ref, v_ref, wo_ref, g_ref, b_ref, out_ref):
    x = x_ref[...]
    q3 = _to_seq_major(_dot(x.astype(BF16), wq_ref[...])).astype(BF16)
    heads = [slice(h * XA_HEAD_DIM, (h + 1) * XA_HEAD_DIM) for h in range(XA_HEADS)]
    units = [(b, sl) for b in range(SCAN_SEQS) for sl in heads]
    s = [_dot_nt(q3[b][:, sl], k_ref[b][:, sl]) * (XA_HEAD_DIM ** -0.5) for b, sl in units]
    p = [_softmax_rows(t).astype(BF16) for t in s]
    pv = [_dot(p[u], v_ref[b][:, sl]) for u, (b, sl) in enumerate(units)]
    per_seq = [jnp.concatenate(pv[b * XA_HEADS:(b + 1) * XA_HEADS], axis=-1) for b in range(SCAN_SEQS)]
    o = _to_time_major(jnp.stack(per_seq, axis=0))
    for rs in _row_halves(o.shape[0]):
        out_ref[rs, :] = _layer_norm(ALPHA * x[rs] + _dot(o[rs].astype(BF16), wo_ref[...]), g_ref[...], b_ref[...])


def _xattn_fused(x, wq, mem_k, mem_v, wo, ln_g, ln_b, layer):
    rows = x.shape[0]
    tm = min(XA_ROWS, rows)
    tile = pl.BlockSpec((tm, D_MODEL), lambda i: (i, 0))
    kv_spec = pl.BlockSpec((None, SCAN_SEQS, MEM_TOKENS, D_MODEL), lambda i: (layer, 0, 0, 0),
                           pipeline_mode=pl.Buffered(1))
    vec = _const_spec((1, D_MODEL))
    return pl.pallas_call(
        _xattn_fused_body, grid=(rows // tm,),
        in_specs=[tile, _layer_spec(*wq), kv_spec, kv_spec, _layer_spec(*wo), vec, vec],
        out_specs=tile, out_shape=jax.ShapeDtypeStruct((rows, D_MODEL), F32),
        compiler_params=_params("parallel"), name="xattn_fused",
    )(x, wq[0], mem_k, mem_v, wo[0], ln_g, ln_b)


def _attn_cache_body(q_ref, k_ref, v_ref, o_ref):
    t_len, n_seq = q_ref.shape[0], q_ref.shape[1]
    shape = (XA_HEADS * t_len, MEM_TOKENS * XA_HEADS)
    own_head = (lax.broadcasted_iota(jnp.int32, shape, 0) // t_len
                == lax.broadcasted_iota(jnp.int32, shape, 1) % XA_HEADS)
    q3 = _to_seq_major(q_ref[...].reshape(t_len * n_seq, D_MODEL))
    outs = []
    for b in range(n_seq):
        q = q3[b]
        qs = jnp.concatenate([q[:, h * XA_HEAD_DIM:(h + 1) * XA_HEAD_DIM] for h in range(XA_HEADS)], axis=0)
        k2 = k_ref[b].reshape(MEM_TOKENS * XA_HEADS, XA_HEAD_DIM).astype(BF16)
        v2 = v_ref[b].reshape(MEM_TOKENS * XA_HEADS, XA_HEAD_DIM).astype(BF16)
        s = _dot_nt(qs.astype(BF16), k2) * (XA_HEAD_DIM ** -0.5)
        p = _softmax_rows(jnp.where(own_head, s, -1e30))
        o = _dot(p.astype(BF16), v2)
        outs.append(jnp.concatenate([o[h * t_len:(h + 1) * t_len] for h in range(XA_HEADS)], axis=-1))
    o_ref[...] = _to_time_major(jnp.stack(outs, axis=0)).reshape(t_len, n_seq, D_MODEL)


def _attention_cache(q_rows, cache_k, cache_v, layer, batch, t_len):
    nb = SCAN_SEQS
    assert t_len == 8 and batch % nb == 0
    kv_spec = pl.BlockSpec((None, nb, MEM_TOKENS, XA_HEADS, XA_HEAD_DIM), lambda b: (layer, b, 0, 0, 0))
    tile = pl.BlockSpec((t_len, nb, D_MODEL), lambda b: (0, b, 0))
    return pl.pallas_call(
        _attn_cache_body, grid=(batch // nb,), in_specs=[tile, kv_spec, kv_spec], out_specs=tile,
        out_shape=jax.ShapeDtypeStruct((t_len, batch, D_MODEL), F32),
        compiler_params=_params("parallel"), name="xattn_cache",
    )(q_rows.reshape(t_len, batch, D_MODEL), cache_k, cache_v).reshape(t_len * batch, D_MODEL)


def _s5_prep_body(lre_ref, lim_ref, ldt_ref, br_ref, bi_ref, are_ref, aim_ref, bbre_ref, bbim_ref):
    lre = lre_ref[...]
    lim = lim_ref[...]
    dt = jnp.exp(ldt_ref[...])
    mag = jnp.exp(lre * dt)
    ang = lim * dt
    ab_re = mag * jnp.cos(ang)
    ab_im = mag * jnp.sin(ang)
    den = lre * lre + lim * lim
    num_re = ab_re - 1.0
    f_re = (num_re * lre + ab_im * lim) / den
    f_im = (ab_im * lre - num_re * lim) / den
    br = br_ref[...]
    bi = bi_ref[...]
    are_ref[...] = ab_re
    aim_ref[...] = ab_im
    bbre_ref[...] = f_re * br - f_im * bi
    bbim_ref[...] = f_re * bi + f_im * br


def _s5_prep(a_re, a_im, log_dt, b_re, b_im, c_re, c_im):
    rep = lambda t: jnp.repeat(t, S5_GROUP, axis=0)
    to_rows = lambda t: jnp.transpose(t, (0, 2, 1)).reshape(S5_GROUPS * S5_GROUP, S5_STATE)
    ins = [rep(a_re), rep(a_im), rep(jnp.broadcast_to(log_dt[:, None], (S5_GROUPS, S5_STATE))),
           to_rows(b_re), to_rows(b_im)]
    shp = jax.ShapeDtypeStruct((S5_GROUPS * S5_GROUP, S5_STATE), F32)
    full = pl.BlockSpec((S5_GROUPS * S5_GROUP, S5_STATE), lambda: (0, 0))
    are, aim, bbre, bbim = pl.pallas_call(
        _s5_prep_body, in_specs=[full] * 5, out_specs=[full] * 4, out_shape=[shp] * 4, name="s5_prep",
    )(*ins)
    a_row = lambda t: t[::S5_GROUP].reshape(1, S5_WIDTH)
    eye = jnp.eye(S5_GROUP, dtype=F32)
    blk_b = lambda t: jnp.einsum('qgcn,gh->qgchn', t.reshape(S5_BLOCKS, 16, S5_GROUP, S5_STATE), eye
                                 ).reshape(S5_BLOCKS, S5_BLOCK_IN, S5_BLOCK_ST)
    wb = jnp.concatenate([blk_b(bbre), blk_b(bbim)], axis=2).astype(BF16)
    blk_c = lambda t: jnp.einsum('qgcn,gh->qhngc', t.reshape(S5_BLOCKS, 16, S5_GROUP, S5_STATE), eye
                                 ).reshape(S5_BLOCKS, S5_BLOCK_ST, S5_BLOCK_IN)
    wc = jnp.concatenate([blk_c(c_re), -blk_c(c_im)], axis=1).astype(BF16)
    return a_row(are), a_row(aim), wb, wc


def _s5_body(batch, x_ref, h0re_ref, h0im_ref, are_ref, aim_ref, wb_ref, wc_ref, d_ref, wv_ref, wg_ref,
             g_ref, b_ref, out_ref, hre_ref, him_ref, hbuf, sre, sim):
    x = _load_rows(x_ref)
    rows = x.shape[0]
    steps = rows // batch

    @pl.when(pl.program_id(0) == 0)
    def _():
        sre[...] = h0re_ref[...]
        sim[...] = h0im_ref[...]

    xb = x.astype(BF16)
    for q in range(S5_BLOCKS):
        hbuf[:, q * 2 * S5_BLOCK_ST:(q + 1) * 2 * S5_BLOCK_ST] = _dot(
            xb[:, q * S5_BLOCK_IN:(q + 1) * S5_BLOCK_IN], wb_ref[q])

    ys = []
    for q in range(S5_BLOCKS):
        st = slice(q * S5_BLOCK_ST, (q + 1) * S5_BLOCK_ST)
        c_re = slice(q * 2 * S5_BLOCK_ST, q * 2 * S5_BLOCK_ST + S5_BLOCK_ST)
        c_im = slice(q * 2 * S5_BLOCK_ST + S5_BLOCK_ST, (q + 1) * 2 * S5_BLOCK_ST)
        ar = jnp.broadcast_to(are_ref[:, st], (8, S5_BLOCK_ST))
        ai = jnp.broadcast_to(aim_ref[:, st], (8, S5_BLOCK_ST))
        for rg in range(batch // 8):
            rs = slice(rg * 8, rg * 8 + 8)
            hr, hi = sre[rs, st], sim[rs, st]
            for t in range(steps):
                row = slice(t * batch + rg * 8, t * batch + rg * 8 + 8)
                hr, hi = (ar * hr - ai * hi + hbuf[row, c_re], ar * hi + ai * hr + hbuf[row, c_im])
                hbuf[row, c_re] = hr
                hbuf[row, c_im] = hi
            sre[rs, st] = hr
            sim[rs, st] = hi
        ys.append(_dot(hbuf[:, q * 2 * S5_BLOCK_ST:(q + 1) * 2 * S5_BLOCK_ST].astype(BF16), wc_ref[q]))
    hre_ref[...] = sre[...]
    him_ref[...] = sim[...]
    y = jnp.concatenate(ys, axis=-1) + d_ref[...] * x
    for rs in _row_halves(rows):
        z = jax.nn.gelu(y[rs]).astype(BF16)
        h = _dot(z, wv_ref[...]) * jax.nn.sigmoid(_dot(z, wg_ref[...]))
        out_ref[rs, :] = _layer_norm(ALPHA * x[rs] + h, g_ref[...], b_ref[...])


def _s5_layer(x, batch, h0re, h0im, s5w, ln_g, ln_b):
    rows = x.size // D_MODEL
    tm = min(ROW_TILE, rows)
    a_re, a_im, wb, wc, d_vec, wv, wg = s5w
    out, hre, him = _rows_call(
        functools.partial(_s5_body, batch), "s5_mixer", rows, tm, [x],
        [h0re, h0im, a_re, a_im, wb, wc, d_vec, wv, wg, ln_g, ln_b], [(D_MODEL, F32)],
        scratch=[pltpu.VMEM((tm, 2 * S5_WIDTH), F32), pltpu.VMEM((batch, S5_WIDTH), F32),
                 pltpu.VMEM((batch, S5_WIDTH), F32)],
        sequential=True, const_outs=[((batch, S5_WIDTH), F32)] * 2)
    return out, hre, him


def _head_sum(v, ones_ref):
    return _dot(v.astype(BF16), ones_ref[...])


def _token_shift_mix(batch, x_ref, shift_ref, mu_ref, xprev):
    rows = x_ref.shape[0]

    @pl.when(pl.program_id(0) == 0)
    def _():
        xprev[...] = shift_ref[...]

    x = x_ref[...]
    if rows > batch:
        xs = jnp.concatenate([xprev[...], x[:rows - batch]], axis=0)
    else:
        xs = xprev[...]
    xprev[...] = x[rows - batch:]
    xx = xs - x
    return lambda i: (x + xx * mu_ref[i:i + 1, :]).astype(BF16)


def _rwkv_proj_rk_body(batch, x_ref, shift_ref, mu_ref, wr_ref, wk_ref, a1_ref, a2_ref, vec_ref, ones_ref,
                       r_out, k_out, a_out, b_out, xprev):
    mix = _token_shift_mix(batch, x_ref, shift_ref, mu_ref, xprev)
    a0, k_k, k_a = vec_ref[1:2, :], vec_ref[3:4, :], vec_ref[4:5, :]
    _store_rows(r_out, _dot(mix(0), wr_ref[...]))
    k = _dot(mix(2), wk_ref[...])
    a = jax.nn.sigmoid(a0 + _dot(_dot(mix(4), a1_ref[...]).astype(BF16), a2_ref[...]))
    kk = k * k_k
    sq = kk * kk
    sq_hi = sq.astype(BF16)
    ss = _dot(sq_hi, ones_ref[...]) + _head_sum(sq - sq_hi.astype(F32), ones_ref)
    kk = kk / jnp.maximum(jnp.sqrt(ss), 1e-12)
    _store_rows(k_out, k * (1.0 + (a - 1.0) * k_a))
    _store_rows(a_out, -kk)
    _store_rows(b_out, kk * a)


def _rwkv_proj_vwg_body(batch, has_vgate, *refs):
    if has_vgate:
        (x_ref, vf_ref, shift_ref, mu_ref, wv_ref, w1_ref, w2_ref, g1_ref, g2_ref, v1_ref, v2_ref, vec_ref,
         lw_out, v_out, g_out, xprev) = refs
    else:
        (x_ref, shift_ref, mu_ref, wv_ref, w1_ref, w2_ref, g1_ref, g2_ref, vec_ref,
         lw_out, v_out, g_out, xprev) = refs
    mix = _token_shift_mix(batch, x_ref, shift_ref, mu_ref, xprev)
    w0, v0 = vec_ref[0:1, :], vec_ref[2:3, :]
    xv = mix(3)
    v = _dot(xv, wv_ref[...])
    w_lora = _dot(jnp.tanh(_dot(mix(1), w1_ref[...])).astype(BF16), w2_ref[...])
    w_log = -jax.nn.softplus(-(w0 + w_lora)) - 0.5
    _store_rows(lw_out, -jnp.exp(w_log))
    if has_vgate:
        gate_v = jax.nn.sigmoid(v0 + _dot(_dot(xv, v1_ref[...]).astype(BF16), v2_ref[...]))
        v = v + (_load_rows(vf_ref) - v) * gate_v
    _store_rows(v_out, v)
    _store_rows(g_out, _dot(jax.nn.sigmoid(_dot(mix(5), g1_ref[...])).astype(BF16), g2_ref[...]))


def _rwkv_consts(seq_len=RWKV_CHUNK):
    L = RWKV_CHUNK
    lane = lax.broadcasted_iota(jnp.int32, (1, PAIR_W), 1)
    t_idx = lax.broadcasted_iota(jnp.int32, (L, PAIR_W), 0)
    s_idx = lax.broadcasted_iota(jnp.int32, (L, PAIR_W), 1) % L
    r2 = lax.broadcasted_iota(jnp.int32, (2 * L, PAIR_W), 0)
    c2 = lax.broadcasted_iota(jnp.int32, (2 * L, PAIR_W), 1)
    rr = lax.broadcasted_iota(jnp.int32, (L, L), 0)
    cc = lax.broadcasted_iota(jnp.int32, (L, L), 1)
    strict_lower, incl_lower, tri = s_idx < t_idx, s_idx <= t_idx, cc <= rr
    cst = {}
    if seq_len < L:
        same_seq = (s_idx // seq_len) == (t_idx // seq_len)
        strict_lower = same_seq & strict_lower
        incl_lower = same_seq & incl_lower
        tri = ((cc // seq_len) == (rr // seq_len)) & tri
        cst["seq_ones"] = jnp.where((cc // seq_len) == (rr // seq_len), 1.0, 0.0).astype(BF16)
        cst["col_seq"] = (c2 % L) // seq_len
    cst.update(
        m0=lane < RWKV_HEAD,
        strict_lower=strict_lower, incl_lower=incl_lower,
        same_head=(r2 < L) == (c2 < RWKV_HEAD),
        eye2=jnp.where(r2 == c2, 1.0, 0.0).astype(F32),
        tri=jnp.where(tri, 1.0, 0.0).astype(BF16),
        zeros_l=jnp.zeros((L, PAIR_W), F32))
    return cst


def _expand_state(sc, m0):
    return jnp.concatenate([jnp.where(m0, sc, 0.0), jnp.where(m0, 0.0, sc)], axis=0)


def _compact_state(s_bd):
    return s_bd[0:RWKV_HEAD] + s_bd[RWKV_HEAD:PAIR_W]


def _rwkv_chunk(slabs, states, cst, seq_len=RWKV_CHUNK):
    L = RWKV_CHUNK
    n_seq = L // seq_len
    m0 = cst["m0"]
    bf = lambda t: t.astype(BF16)
    cat = lambda ts: jnp.concatenate(ts, axis=0)
    ap, rp, vp, btp, ktp, wbp, wkp, dtot = [], [], [], [], [], [], [], []
    for r, lw, k, v, a, b in slabs:
        h1 = bf(lw)
        e1 = lw - h1.astype(F32)
        h2 = bf(e1)
        h3 = bf(e1 - h2.astype(F32))
        cum = _dot(cst["tri"], h1) + _dot(cst["tri"], h2) + _dot(cst["tri"], h3)
        if n_seq == 1:
            tot = cum[L - 1:L, :]
        else:
            tot = _dot(cst["seq_ones"], h1) + _dot(cst["seq_ones"], h2) + _dot(cst["seq_ones"], h3)
        inv = jnp.exp(-cum)
        tail = jnp.exp(tot - cum)
        pieces = (a * jnp.exp(cum - lw), r * jnp.exp(cum), v, b * inv, k * inv, b * tail, k * tail,
                  jnp.exp(tot))
        for p in range(r.shape[1] // PAIR_W):
            sl = slice(p * PAIR_W, (p + 1) * PAIR_W)
            for dst, src in zip((ap, rp, vp, btp, ktp, wbp, wkp, dtot), pieces):
                dst.append(src[:, sl])
    P = range(len(ap))
    gram = [_dot_nt(bf(cat([ap[p], rp[p]])),
                    bf(cat([jnp.where(m0, btp[p], 0.0), jnp.where(m0, ktp[p], 0.0),
                            jnp.where(m0, 0.0, btp[p]), jnp.where(m0, 0.0, ktp[p])]))) for p in P]
    if n_seq == 1:
        ar_s = [_dot_nt(bf(cat([ap[p], rp[p]])), bf(states[p])) for p in P]
    else:
        ar_s = []
        for p in P:
            parts = [_dot_nt(bf(cat([ap[p][s * seq_len:(s + 1) * seq_len], rp[p][s * seq_len:(s + 1) * seq_len]])),
                             bf(states[p * n_seq + s])) for s in range(n_seq)]
            ar_s.append(cat([o[0:seq_len] for o in parts] + [o[seq_len:2 * seq_len] for o in parts]))
    na0 = [jnp.where(cst["strict_lower"], gram[p][0:L, 0:PAIR_W], 0.0) for p in P]
    na1 = [jnp.where(cst["strict_lower"], gram[p][0:L, PAIR_W:2 * PAIR_W], 0.0) for p in P]
    nr = [cat([jnp.where(cst["incl_lower"], gram[p][L:2 * L, 0:PAIR_W], 0.0),
               jnp.where(cst["incl_lower"], gram[p][L:2 * L, PAIR_W:2 * PAIR_W], 0.0)]) for p in P]
    nv = [_dot(bf(cat([na0[p], na1[p]])), bf(cat([cst["zeros_l"], vp[p]]))) for p in P]
    pw = [cat([jnp.where(m0, na0[p], 0.0), jnp.where(m0, 0.0, pltpu.roll(na1[p], RWKV_HEAD, axis=1))])
          for p in P]
    t_inv = [cst["eye2"] + pw[p] for p in P]
    for _ in range(seq_len.bit_length() - 2):
        pwb = [bf(pw[p]) for p in P]
        pw = [_dot(pwb[p], pwb[p]) for p in P]
        t_inv = [t_inv[p] + _dot(bf(t_inv[p]), bf(pw[p])) for p in P]
    u_st = [_dot(bf(t_inv[p]), bf(cat([ar_s[p][0:L], ar_s[p][0:L]]) + nv[p])) for p in P]
    zz = [cat([jnp.where(m0, u_st[p][0:L], u_st[p][L:2 * L]), vp[p]]) for p in P]
    y_st = [_dot(bf(nr[p]), bf(zz[p])) for p in P]
    ys = [ar_s[p][L:2 * L] + jnp.where(m0, y_st[p][0:L], y_st[p][L:2 * L]) for p in P]
    if n_seq == 1:
        upd = [_dot(bf(zz[p].T), bf(cat([wbp[p], wkp[p]]))) for p in P]
        new_states = [states[p] * dtot[p] + jnp.where(cst["same_head"], upd[p], 0.0) for p in P]
    else:
        new_states = []
        for p in P:
            zz_t = zz[p].T
            lhs = cat([jnp.where(cst["col_seq"] == s, zz_t, 0.0) for s in range(n_seq)])
            upd = _dot(bf(lhs), bf(cat([wbp[p], wkp[p]])))
            for s in range(n_seq):
                new_states.append(states[p * n_seq + s] * dtot[p][s * seq_len:s * seq_len + 1]
                                  + jnp.where(cst["same_head"], upd[s * PAIR_W:(s + 1) * PAIR_W], 0.0))
    y_slabs, at = [], 0
    for slab in slabs:
        n_i = slab[0].shape[1] // PAIR_W
        y_slabs.append(jnp.concatenate(ys[at:at + n_i], axis=1) if n_i > 1 else ys[at])
        at += n_i
    return y_slabs, new_states


def _rwkv_scan_short_body(r_ref, lw_ref, k_ref, v_ref, a_ref, b_ref, s0_ref, y_ref, sout_ref):
    t_len, n_seq = r_ref.shape[0], SCAN_SEQS
    cst = _rwkv_consts(t_len)
    slabs, states = [], []
    for g in range(SHORT_GROUPS):
        rows = lambda ref: _to_seq_major(ref[:, g * n_seq:(g + 1) * n_seq, :].reshape(t_len * n_seq, D_MODEL)
                                         ).reshape(n_seq * t_len, D_MODEL)
        slabs.append(tuple(rows(ref) for ref in (r_ref, lw_ref, k_ref, v_ref, a_ref, b_ref)))
        states += [_expand_state(s0_ref[g * n_seq + s, p], cst["m0"])
                   for p in range(RWKV_PAIRS) for s in range(n_seq)]
    ys, new_states = _rwkv_chunk(slabs, states, cst, t_len)
    for g in range(SHORT_GROUPS):
        y_ref[:, g * n_seq:(g + 1) * n_seq, :] = _to_time_major(
            ys[g].reshape(n_seq, t_len, D_MODEL)).reshape(t_len, n_seq, D_MODEL)
        for s in range(n_seq):
            for p in range(RWKV_PAIRS):
                sout_ref[g * n_seq + s, p] = _compact_state(new_states[(g * RWKV_PAIRS + p) * n_seq + s])


def _rwkv_scan_seq_body(r_ref, lw_ref, k_ref, v_ref, a_ref, b_ref, s0_ref, y_ref, sout_ref, s_scr):
    cst = _rwkv_consts()
    in_refs = (r_ref, lw_ref, k_ref, v_ref, a_ref, b_ref)

    @pl.when(pl.program_id(1) == 0)
    def _():
        for b in range(SCAN_SEQS):
            for p in range(SCAN_PAIRS):
                s_scr[b * SCAN_PAIRS + p] = _expand_state(s0_ref[b, p], cst["m0"])

    def sequence_group(i, carry):
        seqs = tuple(SCAN_GROUP * i + n for n in range(SCAN_GROUP))
        slabs = [tuple(ref[s] for ref in in_refs) for s in seqs]
        states = [s_scr[s * SCAN_PAIRS + p] for s in seqs for p in range(SCAN_PAIRS)]
        y_slabs, new_states = _rwkv_chunk(slabs, states, cst)
        for n, s in enumerate(seqs):
            y_ref[s] = y_slabs[n]
            for p in range(SCAN_PAIRS):
                s_scr[s * SCAN_PAIRS + p] = new_states[n * SCAN_PAIRS + p]
        return carry

    lax.fori_loop(0, SCAN_SEQS // SCAN_GROUP, sequence_group, 0)

    @pl.when(pl.program_id(1) == pl.num_programs(1) - 1)
    def _():
        for b in range(SCAN_SEQS):
            for p in range(SCAN_PAIRS):
                sout_ref[b, p] = _compact_state(s_scr[b * SCAN_PAIRS + p])


def _rwkv_out_body(y_ref, r_ref, k_ref, v_ref, g_ref, x_ref, ones_ref, vec_ref, wo_ref, lg_ref, lb_ref, out_ref):
    per_seq = len(y_ref.shape) == 3
    flat = lambda ref: ref[...].reshape(-1, D_MODEL) if per_seq else ref[...]
    x = x_ref[...]
    if per_seq:
        x = _to_seq_major(x).reshape(x.shape)
    y = flat(y_ref)
    r_k, lnx_g, lnx_b = (vec_ref[i:i + 1, :] for i in range(3))
    inv_n = 1.0 / RWKV_HEAD
    mu = _head_sum(y, ones_ref) * inv_n
    yc = y - mu
    var = _head_sum(yc * yc, ones_ref) * inv_n
    yn = yc * lax.rsqrt(var + RWKV_GN_EPS) * lnx_g + lnx_b
    bonus = _head_sum(flat(r_ref) * flat(k_ref) * r_k, ones_ref) * flat(v_ref)
    o = ((yn + bonus) * flat(g_ref)).astype(BF16)
    res = _layer_norm(ALPHA * x + _dot(o, wo_ref[...]), lg_ref[...], lb_ref[...])
    out_ref[...] = _to_time_major(res.reshape(y_ref.shape)) if per_seq else res


def _pair_state(s):
    bsz = s.shape[0]
    s = s.reshape(bsz, RWKV_PAIRS, 2, RWKV_HEAD, RWKV_HEAD)
    return jnp.transpose(s, (0, 1, 3, 2, 4)).reshape(bsz, RWKV_PAIRS, RWKV_HEAD, PAIR_W)


def _unpair_state(s):
    bsz = s.shape[0]
    s = s.reshape(bsz, RWKV_PAIRS, RWKV_HEAD, 2, RWKV_HEAD)
    return jnp.transpose(s, (0, 1, 3, 2, 4)).reshape(bsz, RWKV_HEADS, RWKV_HEAD, RWKV_HEAD)


def _rwkv_layer(x, batch, t_len, s0, shift0, v_first, rw, ln_g, ln_b):
    rows = x.shape[0]
    tm = min(ROW_TILE, rows)
    has_vgate = v_first is not None
    per_seq = batch == SCAN_SEQS
    shift_scratch = [pltpu.VMEM((batch, D_MODEL), F32)]
    r, k, a, b = _rows_call(
        functools.partial(_rwkv_proj_rk_body, batch), "rwkv_proj_rk", rows, tm, [x],
        [shift0, rw["mu"], rw["w_r"], rw["w_k"], rw["a1"], rw["a2"], rw["proj_vec"], rw["ones"]],
        [(D_MODEL, F32)] * 4, scratch=shift_scratch, sequential=True, seq_major_out=per_seq)
    consts = [shift0, rw["mu"], rw["w_v"], rw["w1"], rw["w2"], rw["g1"], rw["g2"]]
    if has_vgate:
        consts += [rw["v1"], rw["v2"]]
    lw, v, g = _rows_call(
        functools.partial(_rwkv_proj_vwg_body, batch, has_vgate), "rwkv_proj_vwg", rows, tm,
        [x] + ([v_first] if has_vgate else []), consts + [rw["proj_vec"]],
        [(D_MODEL, F32)] * 3, scratch=shift_scratch, sequential=True, seq_major_out=per_seq)

    if per_seq:
        assert t_len % RWKV_CHUNK == 0
        n_half = D_MODEL // SCAN_LANES
        tile = pl.BlockSpec((SCAN_SEQS, RWKV_CHUNK, SCAN_LANES), lambda h, c: (0, c, h))
        st_spec = pl.BlockSpec((batch, None, SCAN_PAIRS, RWKV_HEAD, PAIR_W), lambda h, c: (0, h, 0, 0, 0))
        st_shape = (batch, n_half, SCAN_PAIRS, RWKV_HEAD, PAIR_W)
        y, s_last = pl.pallas_call(
            _rwkv_scan_seq_body, grid=(n_half, t_len // RWKV_CHUNK),
            in_specs=[tile] * 6 + [st_spec], out_specs=[tile, st_spec],
            out_shape=[jax.ShapeDtypeStruct((batch, t_len, D_MODEL), F32), jax.ShapeDtypeStruct(st_shape, F32)],
            scratch_shapes=[pltpu.VMEM((SCAN_SEQS * SCAN_PAIRS, PAIR_W, PAIR_W), F32)],
            compiler_params=_params("arbitrary", "arbitrary"), name="rwkv_scan",
        )(r, lw, k, v, a, b, _pair_state(s0).reshape(st_shape))
        s_last = s_last.reshape(batch, RWKV_PAIRS, RWKV_HEAD, PAIR_W)
    else:
        n_seq = SCAN_SEQS * SHORT_GROUPS
        assert t_len * SCAN_SEQS == RWKV_CHUNK and batch % n_seq == 0
        view = lambda t: t.reshape(t_len, batch, D_MODEL)
        tile = pl.BlockSpec((t_len, n_seq, D_MODEL), lambda bi: (0, bi, 0))
        st_spec = pl.BlockSpec((n_seq, RWKV_PAIRS, RWKV_HEAD, PAIR_W), lambda bi: (bi, 0, 0, 0))
        y2, s_last = pl.pallas_call(
            _rwkv_scan_short_body, grid=(batch // n_seq,),
            in_specs=[tile] * 6 + [st_spec], out_specs=[tile, st_spec],
            out_shape=[jax.ShapeDtypeStruct((t_len, batch, D_MODEL), F32),
                       jax.ShapeDtypeStruct((batch, RWKV_PAIRS, RWKV_HEAD, PAIR_W), F32)],
            compiler_params=_params("parallel"), name="rwkv_scan_short",
        )(view(r), view(lw), view(k), view(v), view(a), view(b), _pair_state(s0))
        y = y2.reshape(rows, D_MODEL)

    out = _rows_call(_rwkv_out_body, "rwkv_out", rows, tm, [y, r, k, v, g, x],
                     [rw["ones"], rw["out_vec"], rw["w_o"], ln_g, ln_b], [(D_MODEL, F32)])[0]
    return out, _unpair_state(s_last), x[rows - batch:], (v if not has_vgate else v_first)


def _trunk(x, batch, t_len, mem_k, mem_v, s5_re, s5_im, rwkv_s, shift, w):
    rows = x.size // D_MODEL
    tm = min(ROW_TILE, rows)
    new_re, new_im, new_s, new_shift = [], [], [], []
    v_first = None
    for i in range(DEPTH):
        j = i // 2
        ln = lambda n: (w["ln_g"][i, n][None, :], w["ln_b"][i, n][None, :])
        if i % 2 == 0:
            x, hre, him = _s5_layer(x, batch, s5_re[j].reshape(batch, S5_WIDTH),
                                    s5_im[j].reshape(batch, S5_WIDTH), w["s5"][j], *ln(0))
            new_re.append(hre.reshape(batch, S5_GROUPS, S5_STATE))
            new_im.append(him.reshape(batch, S5_GROUPS, S5_STATE))
        else:
            x, s_last, x_last, v_first = _rwkv_layer(x, batch, t_len, rwkv_s[j], shift[j], v_first,
                                                     w["rwkv"][j], *ln(0))
            new_s.append(s_last)
            new_shift.append(x_last)
        if batch == SCAN_SEQS:
            x = _xattn_fused(x, w["xa_q"][i], mem_k, mem_v, w["xa_o"][i], *ln(1), i)
        else:
            o = _attention_cache(_matmul(x, w["xa_q"][i], "xattn_q"), mem_k, mem_v, i, batch, t_len)
            x = _rows_call(_proj_ln_body, "xattn_out", rows, tm, [x, o],
                           [w["xa_o"][i], *ln(1)], [(D_MODEL, F32)])[0]
        x = _rows_call(_mlp_body, "mlp", rows, tm, [x], [w["mlp1"][i], w["mlp2"][i], *ln(2)], [(D_MODEL, F32)],
                       seq_major_out=(batch == SCAN_SEQS and i == DEPTH - 1))[0]
    return x, jnp.stack(new_re), jnp.stack(new_im), jnp.stack(new_s), jnp.stack(new_shift)


def _time_major(x):
    bsz, t_len, d = x.shape
    return jnp.transpose(x, (1, 0, 2)).reshape(t_len * bsz, d)


def _batch_major(x, bsz, t_len):
    return jnp.transpose(x.reshape(t_len, bsz, -1), (1, 0, 2))


def kernel(x_prompt, x_sample, mem_prompt, cache_mem_k, cache_mem_v, state_s5_re, state_s5_im, state_rwkv, state_shift, ln_g, ln_b, s5_a_re, s5_a_im, s5_log_dt, s5_b_re, s5_b_im, s5_c_re, s5_c_im, s5_d, s5_w_glu_v, s5_w_glu_g, rwkv_mu, rwkv_w_r, rwkv_w_k, rwkv_w_v, rwkv_w_o, rwkv_w0, rwkv_w1, rwkv_w2, rwkv_a0, rwkv_a1, rwkv_a2, rwkv_v0, rwkv_v1, rwkv_v2, rwkv_g1, rwkv_g2, rwkv_k_k, rwkv_k_a, rwkv_r_k, rwkv_lnx_g, rwkv_lnx_b, xa_w_q, xa_w_k, xa_w_v, xa_w_o, mlp_w1, mlp_w2):
    bf = lambda t: t.astype(BF16)
    n_s5, n_rwkv = state_s5_re.shape[0], state_rwkv.shape[0]
    b_p, t_p = x_prompt.shape[0], x_prompt.shape[1]
    b_s, t_s = x_sample.shape[0], x_sample.shape[1]

    head_id = jnp.arange(D_MODEL) // RWKV_HEAD
    ones_bd = (head_id[:, None] == head_id[None, :]).astype(BF16)
    per_layer = lambda stack, n: [(s16, i) for s16 in [bf(stack)] for i in range(n)]
    w = {"ln_g": ln_g, "ln_b": ln_b, "s5": [], "rwkv": [],
         "xa_q": per_layer(xa_w_q, DEPTH), "xa_o": per_layer(xa_w_o, DEPTH),
         "mlp1": per_layer(mlp_w1, DEPTH), "mlp2": per_layer(mlp_w2, DEPTH)}
    glu_v, glu_g = per_layer(s5_w_glu_v, n_s5), per_layer(s5_w_glu_g, n_s5)
    for j in range(n_s5):
        a_re, a_im, wb, wc = _s5_prep(s5_a_re[j], s5_a_im[j], s5_log_dt[j], s5_b_re[j], s5_b_im[j],
                                      s5_c_re[j], s5_c_im[j])
        w["s5"].append((a_re, a_im, wb, wc, s5_d[j][None, :], glu_v[j], glu_g[j]))
    stacks = {name: per_layer(t, n_rwkv) for name, t in (
        ("w_r", rwkv_w_r), ("w_k", rwkv_w_k), ("w_v", rwkv_w_v), ("w_o", rwkv_w_o), ("w1", rwkv_w1),
        ("w2", rwkv_w2), ("a1", rwkv_a1), ("a2", rwkv_a2), ("g1", rwkv_g1), ("g2", rwkv_g2))}
    v1, v2 = per_layer(rwkv_v1, n_rwkv - 1), per_layer(rwkv_v2, n_rwkv - 1)
    for j in range(n_rwkv):
        v0 = rwkv_v0[j - 1] if j > 0 else jnp.zeros((D_MODEL,), F32)
        rw = {name: stack[j] for name, stack in stacks.items()}
        rw.update(mu=rwkv_mu[j], ones=ones_bd,
                  proj_vec=jnp.stack([rwkv_w0[j], rwkv_a0[j], v0, rwkv_k_k[j], rwkv_k_a[j]]),
                  out_vec=jnp.stack([rwkv_r_k[j].reshape(D_MODEL), rwkv_lnx_g[j], rwkv_lnx_b[j]]))
        if j > 0:
            rw["v1"], rw["v2"] = v1[j - 1], v2[j - 1]
        w["rwkv"].append(rw)

    mem_rows = mem_prompt.reshape(b_p * MEM_TOKENS, D_MODEL)
    mem4 = lambda t: t.reshape(DEPTH, b_p, MEM_TOKENS, D_MODEL)
    mk, mk16 = map(mem4, _mem_kv(mem_rows, xa_w_k, "mem_k"))
    mv, mv16 = map(mem4, _mem_kv(mem_rows, xa_w_v, "mem_v"))
    z_s5 = jnp.zeros((n_s5, b_p, S5_GROUPS, S5_STATE), F32)
    z_rwkv = jnp.zeros((n_rwkv, b_p, RWKV_HEADS, RWKV_HEAD, RWKV_HEAD), F32)
    z_shift = jnp.zeros((n_rwkv, b_p, D_MODEL), F32)
    assert b_p == SCAN_SEQS
    y_p, re_p, im_p, s_p, sh_p = _trunk(x_prompt, b_p, t_p, mk16, mv16, z_s5, z_s5, z_rwkv, z_shift, w)

    y_s, re_s, im_s, s_s, sh_s = _trunk(_time_major(x_sample), b_s, t_s, cache_mem_k, cache_mem_v,
                                        state_s5_re, state_s5_im, state_rwkv, state_shift, w)

    kv_shape = (DEPTH, b_p, MEM_TOKENS, XA_HEADS, XA_HEAD_DIM)
    return (y_p, _batch_major(y_s, b_s, t_s), mk.reshape(kv_shape), mv.reshape(kv_shape),
            re_p, im_p, s_p, sh_p, re_s, im_s, s_s, sh_s)
```
